```python
import math
import jax, jax.numpy as jnp
from jax import lax
import numpy as np

D_MODEL = 2048
BATCH = 1
SEQ = 16384
DEPTH = 2

N_META = 16
GRID_W = 64
Q_BLOCK = 128
RMS_EPS = 1e-6
ROPE_THETA = 500000.0
AXIAL_THETA = 10000.0

A_HEADS = 8
A_Q_RANK = 512
A_KV_RANK = 256
A_NOPE = 128
A_ROPE = 64
A_V = 128
A_IN = A_Q_RANK + A_KV_RANK + A_ROPE

B_HEADS = 8
B_KV_HEADS = 2
B_HEAD_DIM = 128
B_IN = (B_HEADS + 2 * B_KV_HEADS) * B_HEAD_DIM

EVEN_IN = A_IN + B_IN
MIX_WIDTH = A_HEADS * A_V + B_HEADS * B_HEAD_DIM

C_HEADS = 32
C_KV_HEADS = 4
C_HEAD_DIM = 64
C_WINDOW = 128
C_ROT = C_HEAD_DIM // 4
C_QKV = (C_HEADS + 2 * C_KV_HEADS) * C_HEAD_DIM

D_FF = 7168
N_EXPERTS = 8
TOP_K = 2
D_EXPERT = 7168
MOE_BLOCK = 256

N_EVEN = (DEPTH + 1) // 2
N_ODD = DEPTH // 2

kernel_name = 'hybrid_mla_axialgqa_swa_moe_encoder'


def rms_norm(x, g):
    xf = x.astype(jnp.float32)
    y = xf * lax.rsqrt(jnp.mean(xf * xf, axis=-1, keepdims=True) + RMS_EPS)
    return (y * g.astype(jnp.float32)).astype(x.dtype)


def rope_tables(pos, dim, theta):
    inv_freq = theta ** (-jnp.arange(0, dim, 2, dtype=jnp.float32) / dim)
    ang = pos.astype(jnp.float32)[:, None] * inv_freq[None, :]
    return jnp.cos(ang), jnp.sin(ang)


def apply_rope(x, cos, sin):
    half = x.shape[-1] // 2
    xf = x.astype(jnp.float32)
    x1, x2 = xf[..., :half], xf[..., half:]
    c = cos[None, :, None, :]
    s = sin[None, :, None, :]
    return jnp.concatenate([x1 * c - x2 * s, x2 * c + x1 * s], axis=-1).astype(x.dtype)


def axial_rope(x, row_cos, row_sin, col_cos, col_sin):
    half = x.shape[-1] // 2
    return jnp.concatenate([apply_rope(x[..., :half], row_cos, row_sin),
                            apply_rope(x[..., half:], col_cos, col_sin)], axis=-1)


def _attend_block(q, k, v, scale):
    s = jnp.einsum('bqhgd,bkhd->bhgqk', q, k, preferred_element_type=jnp.float32) * scale
    p = jax.nn.softmax(s, axis=-1).astype(v.dtype)
    return jnp.einsum('bhgqk,bkhd->bqhgd', p, v)


def dense_attention(q, k, v, scale):
    bsz, L, hkv, grp, _ = q.shape
    S = L - N_META
    nb = S // Q_BLOCK
    out_meta = _attend_block(q[:, :N_META], k, v, scale)
    qb = jnp.moveaxis(q[:, N_META:].reshape(bsz, nb, Q_BLOCK, hkv, grp, -1), 1, 0)
    out = lax.map(lambda qblk: _attend_block(qblk, k, v, scale), qb)
    out = jnp.moveaxis(out, 0, 1).reshape(bsz, S, hkv, grp, -1)
    return jnp.concatenate([out_meta, out], axis=1)


def _sink_attend(q, k, v, valid, sink_l, scale):
    s = jnp.einsum('bqhgd,bkhd->bhgqk', q, k, preferred_element_type=jnp.float32) * scale
    s = jnp.where(valid, s, -jnp.inf)
    m = jnp.maximum(jnp.max(s, axis=-1, keepdims=True), sink_l)
    e = jnp.exp(s - m)
    p = e / (jnp.sum(e, axis=-1, keepdims=True) + jnp.exp(sink_l - m))
    return jnp.einsum('bhgqk,bkhd->bqhgd', p.astype(v.dtype), v)


def window_attention(q, k, v, sink, scale):
    bsz, L, hkv, grp, _ = q.shape
    S = L - N_META
    nb = S // Q_BLOCK
    W = C_WINDOW
    k_meta, v_meta = k[:, :N_META], v[:, :N_META]
    k_real, v_real = k[:, N_META:], v[:, N_META:]
    sink_l = sink.astype(jnp.float32).reshape(hkv, grp)[None, :, :, None, None]
    pq = jnp.arange(N_META)[:, None]
    pk = jnp.concatenate([jnp.arange(N_META), N_META + jnp.arange(W)])[None, :]
    meta_valid = jnp.abs(pq - pk) <= W
    out_meta = _sink_attend(q[:, :N_META],
                            jnp.concatenate([k_meta, k_real[:, :W]], axis=1),
                            jnp.concatenate([v_meta, v_real[:, :W]], axis=1),
                            meta_valid, sink_l, scale)
    padw = ((0, 0), (W, W), (0, 0), (0, 0))
    k_pad = jnp.pad(k_real, padw)
    v_pad = jnp.pad(v_real, padw)
    qb = jnp.moveaxis(q[:, N_META:].reshape(bsz, nb, Q_BLOCK, hkv, grp, -1), 1, 0)
    a = jnp.arange(Q_BLOCK)[:, None]
    c = jnp.arange(Q_BLOCK + 2 * W)[None, :]
    meta_cols = jnp.ones((Q_BLOCK, N_META), dtype=bool)

    def block(args):
        qblk, i = args
        start = i * Q_BLOCK
        kb = lax.dynamic_slice_in_dim(k_pad, start, Q_BLOCK + 2 * W, axis=1)
        vb = lax.dynamic_slice_in_dim(v_pad, start, Q_BLOCK + 2 * W, axis=1)
        kidx = start - W + c
        band_ok = (jnp.abs(start + a - kidx) <= W) & (kidx >= 0) & (kidx < S)
        valid = jnp.concatenate([meta_cols, band_ok], axis=1)
        return _sink_attend(qblk, jnp.concatenate([k_meta, kb], axis=1),
                            jnp.concatenate([v_meta, vb], axis=1), valid, sink_l, scale)

    out = lax.map(block, (qb, jnp.arange(nb)))
    out = jnp.moveaxis(out, 0, 1).reshape(bsz, S, hkv, grp, -1)
    return jnp.concatenate([out_meta, out], axis=1)


def even_mixer(h, a_cos, a_sin, row_cos, row_sin, col_cos, col_sin,
               w_in, a_q_norm, a_wq_b, a_kv_norm, a_wkv_b, b_q_norm, b_k_norm, w_out):
    bsz, L, _ = h.shape
    z = h @ w_in
    cq, ckv, kr, bq, bk, bv = jnp.split(z, [A_Q_RANK, A_Q_RANK + A_KV_RANK, A_IN,
                                            A_IN + B_HEADS * B_HEAD_DIM,
                                            A_IN + (B_HEADS + B_KV_HEADS) * B_HEAD_DIM], axis=-1)
    qa = (rms_norm(cq, a_q_norm) @ a_wq_b).reshape(bsz, L, A_HEADS, A_NOPE + A_ROPE)
    q_rot = apply_rope(qa[..., A_NOPE:], a_cos, a_sin)
    kv = (rms_norm(ckv, a_kv_norm) @ a_wkv_b).reshape(bsz, L, A_HEADS, A_NOPE + A_V)
    k_nope, v_a = kv[..., :A_NOPE], kv[..., A_NOPE:]
    k_rot = apply_rope(kr[:, :, None, :], a_cos, a_sin)
    q_a = jnp.concatenate([qa[..., :A_NOPE], q_rot], axis=-1)[:, :, :, None, :]
    k_a = jnp.concatenate([k_nope, jnp.broadcast_to(k_rot, (bsz, L, A_HEADS, A_ROPE))], axis=-1)
    o_a = dense_attention(q_a, k_a, v_a, 1.0 / math.sqrt(A_NOPE + A_ROPE))
    o_a = o_a.reshape(bsz, L, A_HEADS * A_V)
    q_b = rms_norm(bq.reshape(bsz, L, B_HEADS, B_HEAD_DIM), b_q_norm)
    k_b = rms_norm(bk.reshape(bsz, L, B_KV_HEADS, B_HEAD_DIM), b_k_norm)
    q_b = axial_rope(q_b, row_cos, row_sin, col_cos, col_sin)
    k_b = axial_rope(k_b, row_cos, row_sin, col_cos, col_sin)
    q_b = q_b.reshape(bsz, L, B_KV_HEADS, B_HEADS // B_KV_HEADS, B_HEAD_DIM)
    v_b = bv.reshape(bsz, L, B_KV_HEADS, B_HEAD_DIM)
    o_b = dense_attention(q_b, k_b, v_b, 1.0 / math.sqrt(B_HEAD_DIM))
    o_b = o_b.reshape(bsz, L, B_HEADS * B_HEAD_DIM)
    return jnp.concatenate([o_a, o_b], axis=-1) @ w_out


def odd_mixer(h, c_cos, c_sin, w_qkv, b_qkv, sink, w_out, b_out):
    bsz, L, _ = h.shape
    z = h @ w_qkv + b_qkv
    q, k, v = jnp.split(z, [C_HEADS * C_HEAD_DIM, (C_HEADS + C_KV_HEADS) * C_HEAD_DIM], axis=-1)
    q = q.reshape(bsz, L, C_HEADS, C_HEAD_DIM)
    k = k.reshape(bsz, L, C_KV_HEADS, C_HEAD_DIM)
    v = v.reshape(bsz, L, C_KV_HEADS, C_HEAD_DIM)
    q = jnp.concatenate([apply_rope(q[..., :C_ROT], c_cos, c_sin), q[..., C_ROT:]], axis=-1)
    k = jnp.concatenate([apply_rope(k[..., :C_ROT], c_cos, c_sin), k[..., C_ROT:]], axis=-1)
    q = q.reshape(bsz, L, C_KV_HEADS, C_HEADS // C_KV_HEADS, C_HEAD_DIM)
    o = window_attention(q, k, v, sink, 1.0 / math.sqrt(C_HEAD_DIM))
    return o.reshape(bsz, L, C_HEADS * C_HEAD_DIM) @ w_out + b_out


def swiglu(h, w_gate, w_up, w_down):
    return (jax.nn.silu(h @ w_gate) * (h @ w_up)) @ w_down


def moe_swiglu(h, router, w_gate, w_up, w_down):
    bsz, L, D = h.shape
    tok = h.reshape(-1, D)
    N = tok.shape[0]
    logits = jnp.einsum('nd,de->ne', tok, router, preferred_element_type=jnp.float32)
    top_val, top_idx = lax.top_k(logits, TOP_K)
    gates = jax.nn.softmax(top_val, axis=-1)
    e_flat = top_idx.reshape(-1).astype(jnp.int32)
    t_flat = jnp.repeat(jnp.arange(N, dtype=jnp.int32), TOP_K)
    g_flat = gates.reshape(-1)
    n_assign = N * TOP_K
    order = jnp.argsort(e_flat * n_assign + jnp.arange(n_assign, dtype=jnp.int32))
    e_sorted, t_sorted, g_sorted = e_flat[order], t_flat[order], g_flat[order]
    counts = jnp.bincount(e_flat, length=N_EXPERTS)
    padded = (counts + MOE_BLOCK - 1) // MOE_BLOCK * MOE_BLOCK
    start = jnp.cumsum(counts) - counts
    pend = jnp.cumsum(padded)
    pstart = pend - padded
    dest = pstart[e_sorted] + (jnp.arange(n_assign, dtype=jnp.int32) - start[e_sorted])
    n_rows = -(-(n_assign + N_EXPERTS * (MOE_BLOCK - 1)) // MOE_BLOCK) * MOE_BLOCK
    n_blocks = n_rows // MOE_BLOCK
    row_tok = jnp.zeros((n_rows,), jnp.int32).at[dest].set(t_sorted)
    row_gate = jnp.zeros((n_rows,), jnp.float32).at[dest].set(g_sorted)
    block_expert = jnp.minimum(
        jnp.searchsorted(pend, jnp.arange(n_blocks, dtype=jnp.int32) * MOE_BLOCK, side='right'),
        N_EXPERTS - 1)
    xs = tok[row_tok].reshape(n_blocks, MOE_BLOCK, D)

    def expert_block(args):
        xb, e = args
        return (jax.nn.silu(xb @ w_gate[e]) * (xb @ w_up[e])) @ w_down[e]

    y = lax.map(expert_block, (xs, block_expert)).reshape(n_rows, D)
    y = y * row_gate[:, None].astype(y.dtype)
    out = jnp.zeros_like(tok).at[row_tok].add(y)
    return out.reshape(bsz, L, D)


def setup_inputs(seed: int = 0) -> dict:
    key = jax.random.key(seed)
    k = jax.random.split(key, 32)
    f32 = jnp.float32

    def w(kk, shape, fan_in):
        return jax.random.normal(kk, shape, f32) * (fan_in ** -0.5)

    def gain(kk, shape):
        return 1.0 + 0.02 * jax.random.normal(kk, shape, f32)

    E, O = N_EVEN, N_ODD
    return {
        'x': jax.random.normal(k[0], (BATCH, SEQ, D_MODEL), f32),
        'meta_tokens': jax.random.normal(k[1], (N_META, D_MODEL), f32),
        'even_attn_pre': gain(k[2], (E, D_MODEL)),
        'even_attn_post': gain(k[3], (E, D_MODEL)),
        'even_w_in': w(k[4], (E, D_MODEL, EVEN_IN), D_MODEL),
        'a_q_norm': gain(k[5], (E, A_Q_RANK)),
        'a_wq_b': w(k[6], (E, A_Q_RANK, A_HEADS * (A_NOPE + A_ROPE)), A_Q_RANK),
        'a_kv_norm': gain(k[7], (E, A_KV_RANK)),
        'a_wkv_b': w(k[8], (E, A_KV_RANK, A_HEADS * (A_NOPE + A_V)), A_KV_RANK),
        'b_q_norm': gain(k[9], (E, B_HEAD_DIM)),
        'b_k_norm': gain(k[10], (E, B_HEAD_DIM)),
        'even_w_out': w(k[11], (E, MIX_WIDTH, D_MODEL), MIX_WIDTH),
        'even_ffn_pre': gain(k[12], (E, D_MODEL)),
        'even_ffn_post': gain(k[13], (E, D_MODEL)),
        'ffn_w_gate': w(k[14], (E, D_MODEL, D_FF), D_MODEL),
        'ffn_w_up': w(k[15], (E, D_MODEL, D_FF), D_MODEL),
        'ffn_w_down': w(k[16], (E, D_FF, D_MODEL), D_FF),
        'odd_attn_pre': gain(k[17], (O, D_MODEL)),
        'odd_attn_post': gain(k[18], (O, D_MODEL)),
        'c_w_qkv': w(k[19], (O, D_MODEL, C_QKV), D_MODEL),
        'c_b_qkv': 0.02 * jax.random.normal(k[20], (O, C_QKV), f32),
        'c_sink': jax.random.normal(k[21], (O, C_HEADS), f32),
        'c_w_out': w(k[22], (O, C_HEADS * C_HEAD_DIM, D_MODEL), C_HEADS * C_HEAD_DIM),
        'c_b_out': 0.02 * jax.random.normal(k[23], (O, D_MODEL), f32),
        'odd_ffn_pre': gain(k[24], (O, D_MODEL)),
        'odd_ffn_post': gain(k[25], (O, D_MODEL)),
        'moe_router': w(k[26], (O, D_MODEL, N_EXPERTS), D_MODEL),
        'moe_w_gate': w(k[27], (O, N_EXPERTS, D_MODEL, D_EXPERT), D_MODEL),
        'moe_w_up': w(k[28], (O, N_EXPERTS, D_MODEL, D_EXPERT), D_MODEL),
        'moe_w_down': w(k[29], (O, N_EXPERTS, D_EXPERT, D_MODEL), D_EXPERT),
    }


def reference(x, meta_tokens, even_attn_pre, even_attn_post, even_w_in, a_q_norm, a_wq_b,
              a_kv_norm, a_wkv_b, b_q_norm, b_k_norm, even_w_out, even_ffn_pre, even_ffn_post,
              ffn_w_gate, ffn_w_up, ffn_w_down, odd_attn_pre, odd_attn_post, c_w_qkv, c_b_qkv,
              c_sink, c_w_out, c_b_out, odd_ffn_pre, odd_ffn_post, moe_router, moe_w_gate,
              moe_w_up, moe_w_down):
    bsz, S, D = x.shape
    L = N_META + S
    h = jnp.concatenate([jnp.broadcast_to(meta_tokens.astype(x.dtype)[None], (bsz, N_META, D)), x], axis=1)
    pos = jnp.arange(L, dtype=jnp.int32)
    a_cos, a_sin = rope_tables(pos, A_ROPE, ROPE_THETA)
    c_cos, c_sin = rope_tables(pos, C_ROT, ROPE_THETA)
    rows = S // GRID_W
    grid_row = jnp.concatenate([jnp.full((N_META,), -1, jnp.int32),
                                jnp.repeat(jnp.arange(rows, dtype=jnp.int32), GRID_W)])
    grid_col = jnp.concatenate([jnp.arange(N_META, dtype=jnp.int32),
                                jnp.tile(jnp.arange(GRID_W, dtype=jnp.int32), rows)])
    row_cos, row_sin = rope_tables(grid_row, B_HEAD_DIM // 2, AXIAL_THETA)
    col_cos, col_sin = rope_tables(grid_col, B_HEAD_DIM // 2, AXIAL_THETA)

    for layer in range(DEPTH):
        i = layer // 2
        if layer % 2 == 0:
            y = even_mixer(rms_norm(h, even_attn_pre[i]), a_cos, a_sin, row_cos, row_sin,
                           col_cos, col_sin, even_w_in[i], a_q_norm[i], a_wq_b[i], a_kv_norm[i],
                           a_wkv_b[i], b_q_norm[i], b_k_norm[i], even_w_out[i])
            h = h + rms_norm(y, even_attn_post[i])
            y = swiglu(rms_norm(h, even_ffn_pre[i]), ffn_w_gate[i], ffn_w_up[i], ffn_w_down[i])
            h = h + rms_norm(y, even_ffn_post[i])
        else:
            y = odd_mixer(rms_norm(h, odd_attn_pre[i]), c_cos, c_sin, c_w_qkv[i], c_b_qkv[i],
                          c_sink[i], c_w_out[i], c_b_out[i])
            h = h + rms_norm(y, odd_attn_post[i])
            y = moe_swiglu(rms_norm(h, odd_ffn_pre[i]), moe_router[i], moe_w_gate[i],
                           moe_w_up[i], moe_w_down[i])
            h = h + rms_norm(y, odd_ffn_post[i])
    return h[:, N_META:]
```

```python
import functools
import math

import jax
import jax.numpy as jnp
from jax import lax
from jax.experimental import pallas as pl
from jax.experimental.pallas import tpu as pltpu

F32 = jnp.float32
BF16 = jnp.bfloat16

N_META = 16
GRID_W = 64
RMS_EPS = 1e-6
ROPE_THETA = 500000.0
AXIAL_THETA = 10000.0
A_HEADS, A_Q_RANK, A_KV_RANK, A_NOPE, A_ROPE, A_V = 8, 512, 256, 128, 64, 128
A_IN = A_Q_RANK + A_KV_RANK + A_ROPE
B_HEADS, B_KV_HEADS, B_HEAD_DIM = 8, 2, 128
C_HEADS, C_KV_HEADS, C_HEAD_DIM, C_WINDOW = 32, 4, 64, 128
C_ROT = C_HEAD_DIM // 4
N_EXPERTS, TOP_K = 8, 2

LANES = 128
V7X_VMEM_LIMIT = 56 * 1024 * 1024
LOG2E = math.log2(math.e)

ROW_PAD = 256
A_QK = 2 * LANES


def _cparams(sem):
    return pltpu.CompilerParams(dimension_semantics=sem, vmem_limit_bytes=V7X_VMEM_LIMIT)


def _row_tile(n_rows, target):
    best = LANES
    t = LANES
    while t <= min(n_rows, target):
        if n_rows % t == 0:
            best = t
        t += LANES
    return best


def _rms(x, g):
    ms = jnp.mean(x * x, axis=-1, keepdims=True)
    return x * lax.rsqrt(ms + RMS_EPS) * g


def _dot(a, b):
    return jnp.dot(a, b, preferred_element_type=F32)


def _dot_nt(a, b):
    return lax.dot_general(a, b, (((1,), (1,)), ((), ())), preferred_element_type=F32)


def _half_swap(x, half):
    lane = lax.broadcasted_iota(jnp.int32, x.shape, x.ndim - 1)
    up = pltpu.roll(x, x.shape[-1] - half, x.ndim - 1)
    down = pltpu.roll(x, half, x.ndim - 1)
    return jnp.where((lane & half) == 0, up, down)


def _prep_kernel(x_ref, meta_ref, g_ref, h_ref, xn_ref, *, n_real_blocks):
    i = pl.program_id(0)

    @pl.when(i < n_real_blocks)
    def _():
        h = x_ref[...]
        h_ref[...] = h
        xn_ref[...] = _rms(h, g_ref[...]).astype(xn_ref.dtype)

    @pl.when(i >= n_real_blocks)
    def _():
        h_ref[...] = jnp.zeros_like(h_ref)
        xn_ref[...] = jnp.zeros_like(xn_ref)
        m = meta_ref[...]
        h_ref[0:N_META, :] = m
        xn_ref[0:N_META, :] = _rms(m, g_ref[...]).astype(xn_ref.dtype)


def _prep(x2d, meta, g, lp):
    s, d = x2d.shape
    tm = ROW_PAD
    nrb = s // tm
    return pl.pallas_call(
        functools.partial(_prep_kernel, n_real_blocks=nrb),
        out_shape=(jax.ShapeDtypeStruct((lp, d), F32), jax.ShapeDtypeStruct((lp, d), BF16)),
        grid=(lp // tm,),
        in_specs=[pl.BlockSpec((tm, d), lambda i: (jnp.minimum(i, nrb - 1), 0)),
                  pl.BlockSpec((N_META, d), lambda i: (0, 0)),
                  pl.BlockSpec((1, d), lambda i: (0, 0))],
        out_specs=(pl.BlockSpec((tm, d), lambda i: (i, 0)), pl.BlockSpec((tm, d), lambda i: (i, 0))),
        compiler_params=_cparams(("parallel",)),
        name="prep",
    )(x2d, meta, g)


def _even_proj_kernel(xn_ref, w_in_ref, gq_ref, wq_ref, gkv_ref, wkv_ref, bgq_ref, bgk_ref,
                      acos_ref, asin_ref, bcos_ref, bsin_ref,
                      qa_ref, kva_ref, krot_ref, qb_ref, kb_ref, vb_ref):
    z = _dot(xn_ref[...], w_in_ref[...])
    acos, asin = acos_ref[...], asin_ref[...]
    bcos, bsin = bcos_ref[...], bsin_ref[...]
    sa = LOG2E / math.sqrt(A_NOPE + A_ROPE)
    sb = LOG2E / math.sqrt(B_HEAD_DIM)
    o = 0
    cqn = _rms(z[:, o:o + A_Q_RANK], gq_ref[...]).astype(BF16)
    o += A_Q_RANK
    qa = _dot(cqn, wq_ref[...])
    for h in range(A_HEADS):
        c = h * A_QK
        qa_ref[:, c:c + LANES] = (qa[:, c:c + LANES] * sa).astype(BF16)
        rot = qa[:, c + LANES:c + A_QK]
        rot = rot * acos + _half_swap(rot, A_ROPE // 2) * asin
        qa_ref[:, c + LANES:c + A_QK] = (rot * sa).astype(BF16)
    ckvn = _rms(z[:, o:o + A_KV_RANK], gkv_ref[...]).astype(BF16)
    o += A_KV_RANK
    kva_ref[...] = _dot(ckvn, wkv_ref[...]).astype(BF16)
    kr = z[:, o:o + LANES]
    o += LANES
    krot_ref[...] = (kr * acos + _half_swap(kr, A_ROPE // 2) * asin).astype(BF16)

    def norm_rope(x, g):
        xn = _rms(x, g)
        return xn * bcos + _half_swap(xn, B_HEAD_DIM // 4) * bsin

    for h in range(B_HEADS):
        c = h * B_HEAD_DIM
        qb_ref[:, c:c + B_HEAD_DIM] = (norm_rope(z[:, o + c:o + c + B_HEAD_DIM], bgq_ref[...]) * sb).astype(BF16)
    o += B_HEADS * B_HEAD_DIM
    for h in range(B_KV_HEADS):
        c = h * B_HEAD_DIM
        kb_ref[:, c:c + B_HEAD_DIM] = norm_rope(z[:, o + c:o + c + B_HEAD_DIM], bgk_ref[...]).astype(BF16)
    o += B_KV_HEADS * B_HEAD_DIM
    vb_ref[...] = z[:, o:o + B_KV_HEADS * B_HEAD_DIM].astype(BF16)


def _even_proj(xn, w_in_p, gq, wq_p, gkv, wkv, bgq, bgk, acos, asin, bcos, bsin, tm):
    lp, d = xn.shape
    n_in = w_in_p.shape[1]
    row = lambda n: pl.BlockSpec((tm, n), lambda i: (i, 0))
    full = lambda a: pl.BlockSpec(a.shape, lambda i: (0, 0))
    widths = (A_HEADS * A_QK, A_HEADS * (A_NOPE + A_V), LANES, B_HEADS * B_HEAD_DIM,
              B_KV_HEADS * B_HEAD_DIM, B_KV_HEADS * B_HEAD_DIM)
    return pl.pallas_call(
        _even_proj_kernel,
        out_shape=tuple(jax.ShapeDtypeStruct((lp, n), BF16) for n in widths),
        grid=(lp // tm,),
        in_specs=[row(d), full(w_in_p), full(gq), full(wq_p), full(gkv), full(wkv), full(bgq), full(bgk),
                  row(LANES), row(LANES), row(LANES), row(LANES)],
        out_specs=tuple(row(n) for n in widths),
        compiler_params=_cparams(("parallel",)),
        name="even_proj",
    )(xn, w_in_p, gq, wq_p, gkv, wkv, bgq, bgk, acos, asin, bcos, bsin)


def _flash_body(q, kv_chunk, kv_meta, o_ref, m_sc, l_sc, acc_sc, *, n_chunks, n_stack, dv):
    km, vm = kv_meta()
    s = _dot_nt(q, km)
    m = jnp.max(s, axis=1, keepdims=True)
    p = jnp.exp2(s - m)
    m_sc[...] = m
    l_sc[...] = jnp.sum(p, axis=1, keepdims=True)
    acc_sc[...] = _dot(p.astype(BF16), vm)

    def step(c, carry):
        k, v = kv_chunk(c)
        s = _dot_nt(q, k)
        m_prev = m_sc[...]
        m_new = jnp.maximum(m_prev, jnp.max(s, axis=1, keepdims=True))
        alpha = jnp.exp2(m_prev - m_new)
        p = jnp.exp2(s - m_new)
        l_sc[...] = alpha * l_sc[...] + jnp.sum(p, axis=1, keepdims=True)
        acc_sc[...] = alpha * acc_sc[...] + _dot(p.astype(BF16), v)
        m_sc[...] = m_new
        return carry

    lax.fori_loop(0, n_chunks, step, 0)
    out = acc_sc[...] / l_sc[...]
    tq = out.shape[0] // n_stack
    for j in range(n_stack):
        o_ref[:, j * dv:(j + 1) * dv] = out[j * tq:(j + 1) * tq].astype(o_ref.dtype)


def _mla_attn_kernel(q_ref, kn_ref, kr_ref, v_ref, knm_ref, krm_ref, vm_ref, o_ref, m_sc, l_sc, acc_sc, *, tk):
    n_chunks = kn_ref.shape[0] // tk

    def kv_chunk(c):
        rows = pl.ds(pl.multiple_of(c * tk, tk), tk)
        return jnp.concatenate([kn_ref[rows, :], kr_ref[rows, :]], axis=1), v_ref[rows, :]

    def kv_meta():
        return jnp.concatenate([knm_ref[...], krm_ref[...]], axis=1), vm_ref[...]

    _flash_body(q_ref[...], kv_chunk, kv_meta, o_ref, m_sc, l_sc, acc_sc, n_chunks=n_chunks, n_stack=1, dv=A_V)


def _mla_attention(qa, kva, krot, s_real, tq, tk):
    lp = qa.shape[0]
    mb = s_real // N_META
    return pl.pallas_call(
        functools.partial(_mla_attn_kernel, tk=tk),
        out_shape=jax.ShapeDtypeStruct((lp, A_HEADS * A_V), BF16),
        grid=(A_HEADS, lp // tq),
        in_specs=[pl.BlockSpec((tq, A_QK), lambda h, i: (i, h)),
                  pl.BlockSpec((s_real, LANES), lambda h, i: (0, 2 * h)),
                  pl.BlockSpec((s_real, LANES), lambda h, i: (0, 0)),
                  pl.BlockSpec((s_real, LANES), lambda h, i: (0, 2 * h + 1)),
                  pl.BlockSpec((N_META, LANES), lambda h, i: (mb, 2 * h)),
                  pl.BlockSpec((N_META, LANES), lambda h, i: (mb, 0)),
                  pl.BlockSpec((N_META, LANES), lambda h, i: (mb, 2 * h + 1))],
        out_specs=pl.BlockSpec((tq, A_V), lambda h, i: (i, h)),
        scratch_shapes=[pltpu.VMEM((tq, 1), F32), pltpu.VMEM((tq, 1), F32), pltpu.VMEM((tq, A_V), F32)],
        compiler_params=_cparams(("parallel", "parallel")),
        name="mla_attention",
    )(qa, kva, krot, kva, kva, krot, kva)


def _gqa_attn_kernel(q_ref, k_ref, v_ref, km_ref, vm_ref, o_ref, m_sc, l_sc, acc_sc, *, tk, n_stack):
    n_chunks = k_ref.shape[0] // tk
    dk = k_ref.shape[1]
    q = jnp.concatenate([q_ref[:, j * dk:(j + 1) * dk] for j in range(n_stack)], axis=0)

    def kv_chunk(c):
        rows = pl.ds(pl.multiple_of(c * tk, tk), tk)
        return k_ref[rows, :], v_ref[rows, :]

    def kv_meta():
        return km_ref[...], vm_ref[...]

    _flash_body(q, kv_chunk, kv_meta, o_ref, m_sc, l_sc, acc_sc, n_chunks=n_chunks, n_stack=n_stack,
                dv=v_ref.shape[1])


def _gqa_attention(qb, kb, vb, s_real, tq, tk):
    lp = qb.shape[0]
    grp = B_HEADS // B_KV_HEADS
    dh = B_HEAD_DIM
    mb = s_real // N_META
    return pl.pallas_call(
        functools.partial(_gqa_attn_kernel, tk=tk, n_stack=grp),
        out_shape=jax.ShapeDtypeStruct((lp, B_HEADS * dh), BF16),
        grid=(B_KV_HEADS, lp // tq),
        in_specs=[pl.BlockSpec((tq, grp * dh), lambda g, i: (i, g)),
                  pl.BlockSpec((s_real, dh), lambda g, i: (0, g)),
                  pl.BlockSpec((s_real, dh), lambda g, i: (0, g)),
                  pl.BlockSpec((N_META, dh), lambda g, i: (mb, g)),
                  pl.BlockSpec((N_META, dh), lambda g, i: (mb, g))],
        out_specs=pl.BlockSpec((tq, grp * dh), lambda g, i: (i, g)),
        scratch_shapes=[pltpu.VMEM((grp * tq, 1), F32), pltpu.VMEM((grp * tq, 1), F32),
                        pltpu.VMEM((grp * tq, dh), F32)],
        compiler_params=_cparams(("parallel", "parallel")),
        name="gqa_attention",
    )(qb, kb, vb, kb, vb)


def _out_proj0_kernel(oa_ref, ob_ref, wa_ref, wb_ref, h_ref, gpost_ref, gnext_ref, hn_ref, xn_ref):
    y = _dot(oa_ref[...], wa_ref[...]) + _dot(ob_ref[...], wb_ref[...])
    h = h_ref[...] + _rms(y, gpost_ref[...])
    hn_ref[...] = h
    xn_ref[...] = _rms(h, gnext_ref[...]).astype(xn_ref.dtype)


def _out_proj0(oa, ob, wa, wb, h, gpost, gnext, tm):
    lp, d = h.shape
    row = lambda n: pl.BlockSpec((tm, n), lambda i: (i, 0))
    full = lambda a: pl.BlockSpec(a.shape, lambda i: (0, 0))
    return pl.pallas_call(
        _out_proj0_kernel,
        out_shape=(jax.ShapeDtypeStruct((lp, d), F32), jax.ShapeDtypeStruct((lp, d), BF16)),
        grid=(lp // tm,),
        in_specs=[row(oa.shape[1]), row(ob.shape[1]), full(wa), full(wb), row(d), full(gpost), full(gnext)],
        out_specs=(row(d), row(d)),
        compiler_params=_cparams(("parallel",)),
        name="out_proj0",
    )(oa, ob, wa, wb, h, gpost, gnext)


def _swiglu_partial(x, wg_ref, wu_ref, wd_ref):
    g = _dot(x, wg_ref[...].astype(BF16))
    u = _dot(x, wu_ref[...].astype(BF16))
    a = (g * jax.nn.sigmoid(g) * u).astype(BF16)
    return _dot(a, wd_ref[...].astype(BF16))


def _ffn0_kernel(x_ref, wg_ref, wu_ref, wd_ref, h_ref, gpost_ref, gnext_ref, hn_ref, xn_ref):
    j = pl.program_id(1)
    part = _swiglu_partial(x_ref[...], wg_ref, wu_ref, wd_ref)

    @pl.when(j == 0)
    def _():
        hn_ref[...] = part

    @pl.when(j > 0)
    def _():
        hn_ref[...] += part

    @pl.when(j == pl.num_programs(1) - 1)
    def _():
        h = h_ref[...] + _rms(hn_ref[...], gpost_ref[...])
        hn_ref[...] = h
        xn_ref[...] = _rms(h, gnext_ref[...]).astype(xn_ref.dtype)


def _ffn0(xn, wg, wu, wd, h, gpost, gnext, tm, tf):
    lp, d = h.shape
    f = wg.shape[1]
    row = pl.BlockSpec((tm, d), lambda i, j: (i, 0))
    vec = pl.BlockSpec((1, d), lambda i, j: (0, 0))
    return pl.pallas_call(
        _ffn0_kernel,
        out_shape=(jax.ShapeDtypeStruct((lp, d), F32), jax.ShapeDtypeStruct((lp, d), BF16)),
        grid=(lp // tm, f // tf),
        in_specs=[row, pl.BlockSpec((d, tf), lambda i, j: (0, j)), pl.BlockSpec((d, tf), lambda i, j: (0, j)),
                  pl.BlockSpec((tf, d), lambda i, j: (j, 0)), row, vec, vec],
        out_specs=(row, row),
        compiler_params=_cparams(("parallel", "arbitrary")),
        name="ffn0",
    )(xn, wg, wu, wd, h, gpost, gnext)


def _odd_proj_kernel(xn_ref, w_ref, b_ref, cos_ref, sin_ref, q_ref, k_ref, v_ref):
    z = _dot(xn_ref[...], w_ref[...]) + b_ref[...]
    cos, sin = cos_ref[...], sin_ref[...]
    sc = LOG2E / math.sqrt(C_HEAD_DIM)
    nq = q_ref.shape[1]
    nk = k_ref.shape[1]

    def rope(x):
        return x * cos + _half_swap(x, C_ROT // 2) * sin

    for c in range(0, nq, LANES):
        q_ref[:, c:c + LANES] = (rope(z[:, c:c + LANES]) * sc).astype(BF16)
    for c in range(0, nk, LANES):
        k_ref[:, c:c + LANES] = rope(z[:, nq + c:nq + c + LANES]).astype(BF16)
    v_ref[...] = z[:, nq + nk:].astype(BF16)


def _odd_proj(xn, w_p, b_p, cos, sin, tm):
    lp, d = xn.shape
    nq = C_HEADS * C_HEAD_DIM
    nk = C_KV_HEADS * LANES
    row = lambda n: pl.BlockSpec((tm, n), lambda i: (i, 0))
    full = lambda a: pl.BlockSpec(a.shape, lambda i: (0, 0))
    return pl.pallas_call(
        _odd_proj_kernel,
        out_shape=(jax.ShapeDtypeStruct((lp, nq), BF16), jax.ShapeDtypeStruct((lp, nk), BF16),
                   jax.ShapeDtypeStruct((lp, nk), BF16)),
        grid=(lp // tm,),
        in_specs=[row(d), full(w_p), full(b_p), row(LANES), row(LANES)],
        out_specs=(row(nq), row(nk), row(nk)),
        compiler_params=_cparams(("parallel",)),
        name="odd_proj",
    )(xn, w_p, b_p, cos, sin)


def _win_attn_kernel(sink_ref, q_ref, k0, k1, k2, k3, km, v0, v1, v2, v3, vm, o_ref, *, s_real, tq):
    g = pl.program_id(0)
    i = pl.program_id(1)
    is_meta = i == pl.num_programs(1) - 1
    nband = 4 * LANES
    kcat = jnp.concatenate([k0[...], k1[...], k2[...], k3[...], km[...]], axis=0)
    vcat = jnp.concatenate([v0[...], v1[...], v2[...], v3[...], vm[...]], axis=0)
    ncol = kcat.shape[0]
    a = lax.broadcasted_iota(jnp.int32, (tq, ncol), 0)
    c = lax.broadcasted_iota(jnp.int32, (tq, ncol), 1)
    kidx = jnp.where(is_meta, 0, i * tq - LANES) + c
    qidx = i * tq + a
    ninf = jnp.float32(-jnp.inf)
    b_real = jnp.where(jnp.abs(qidx - kidx) <= C_WINDOW, 0.0, ninf)
    b_real = jnp.where(kidx >= 0, b_real, ninf)
    b_real = jnp.where(kidx < s_real, b_real, ninf)
    b_meta_q = jnp.where(kidx <= (C_WINDOW - N_META) + a, 0.0, ninf)
    b_meta_k = jnp.where(c - nband < N_META, 0.0, ninf)
    bias = jnp.where(c < nband, jnp.where(is_meta, b_meta_q, b_real), b_meta_k)
    lane = lax.broadcasted_iota(jnp.int32, (tq, LANES), 1)
    heads_per_group = C_HEADS // C_KV_HEADS
    for pr in range(heads_per_group // 2):
        qp = q_ref[:, pr * LANES:(pr + 1) * LANES]
        halves = []
        for half in range(2):
            sink = sink_ref[g * heads_per_group + 2 * pr + half] * LOG2E
            qm = jnp.where((lane >= C_HEAD_DIM) == (half == 1), qp, jnp.zeros_like(qp))
            s = _dot_nt(qm, kcat) + bias
            m = jnp.maximum(jnp.max(s, axis=1, keepdims=True), sink)
            e = jnp.exp2(s - m)
            denom = jnp.sum(e, axis=1, keepdims=True) + jnp.exp2(sink - m)
            halves.append(_dot(e.astype(BF16), vcat) / denom)
        o_ref[:, pr * LANES:(pr + 1) * LANES] = jnp.where(lane < C_HEAD_DIM, halves[0], halves[1]).astype(o_ref.dtype)


def _win_attention(qc, kc, vc, sink, s_real, tq):
    lp = qc.shape[0]
    nq = lp // tq
    nkb = s_real // LANES
    per = tq // LANES
    gw = (C_HEADS // C_KV_HEADS) * C_HEAD_DIM

    def band(k):
        def index(g, i, sink_ref):
            b = jnp.where(i == nq - 1, k, jnp.clip(i * per - 1 + k, 0, nkb - 1))
            return (b, g)
        return pl.BlockSpec((LANES, LANES), index)

    meta = pl.BlockSpec((LANES, LANES), lambda g, i, sink_ref: (nkb, g))
    grid_spec = pltpu.PrefetchScalarGridSpec(
        num_scalar_prefetch=1,
        grid=(C_KV_HEADS, nq),
        in_specs=[pl.BlockSpec((tq, gw), lambda g, i, sink_ref: (i, g))]
                 + [band(k) for k in range(4)] + [meta] + [band(k) for k in range(4)] + [meta],
        out_specs=pl.BlockSpec((tq, gw), lambda g, i, sink_ref: (i, g)),
    )
    return pl.pallas_call(
        functools.partial(_win_attn_kernel, s_real=s_real, tq=tq),
        out_shape=jax.ShapeDtypeStruct((lp, C_HEADS * C_HEAD_DIM), BF16),
        grid_spec=grid_spec,
        compiler_params=_cparams(("parallel", "parallel")),
        name="win_attention",
    )(sink, qc, kc, kc, kc, kc, kc, vc, vc, vc, vc, vc)


def _out_proj1_kernel(o_ref, w_ref, b_ref, h_ref, gpost_ref, gnext_ref, router_ref, hn_ref, xn_ref, idx_ref, gate_ref):
    y = _dot(o_ref[...], w_ref[...]) + b_ref[...]
    h = h_ref[...] + _rms(y, gpost_ref[...])
    hn_ref[...] = h
    xn = _rms(h, gnext_ref[...])
    xn_ref[...] = xn
    logits = jnp.dot(xn, router_ref[...], preferred_element_type=F32, precision=lax.Precision.HIGHEST)
    lane = lax.broadcasted_iota(jnp.int32, logits.shape, 1).astype(F32)
    logits = jnp.where(lane < N_EXPERTS, logits, -jnp.inf)
    v1 = jnp.max(logits, axis=1, keepdims=True)
    i1 = jnp.min(jnp.where(logits == v1, lane, float(LANES)), axis=1, keepdims=True)
    rest = jnp.where(lane == i1, -jnp.inf, logits)
    v2 = jnp.max(rest, axis=1, keepdims=True)
    i2 = jnp.min(jnp.where(rest == v2, lane, float(LANES)), axis=1, keepdims=True)
    e2 = jnp.exp(v2 - v1)
    g1 = 1.0 / (1.0 + e2)
    g2 = e2 / (1.0 + e2)
    idx_ref[...] = jnp.where(lane == 0, i1, jnp.where(lane == 1, i2, 0.0)).astype(jnp.int32)
    gate_ref[...] = jnp.where(lane == 0, g1, jnp.where(lane == 1, g2, 0.0))


def _out_proj1(o, w, b, h, gpost, gnext, router_p, tm):
    lp, d = h.shape
    row = lambda n: pl.BlockSpec((tm, n), lambda i: (i, 0))
    full = lambda a: pl.BlockSpec(a.shape, lambda i: (0, 0))
    return pl.pallas_call(
        _out_proj1_kernel,
        out_shape=(jax.ShapeDtypeStruct((lp, d), F32), jax.ShapeDtypeStruct((lp, d), F32),
                   jax.ShapeDtypeStruct((lp, LANES), jnp.int32), jax.ShapeDtypeStruct((lp, LANES), F32)),
        grid=(lp // tm,),
        in_specs=[row(o.shape[1]), full(w), full(b), row(d), full(gpost), full(gnext), full(router_p)],
        out_specs=(row(d), row(d), row(LANES), row(LANES)),
        compiler_params=_cparams(("parallel",)),
        name="out_proj1",
    )(o, w, b, h, gpost, gnext, router_p)


def _gather_kernel(tok_ref, x_hbm, o_ref, buf, sem, *, tg):
    base = pl.program_id(0) * tg

    def row_copy(r):
        return pltpu.make_async_copy(x_hbm.at[pl.ds(tok_ref[base + r], 1)], buf.at[pl.ds(r, 1)], sem)

    def start(r, c):
        row_copy(r).start()
        return c

    def wait(r, c):
        row_copy(r).wait()
        return c

    lax.fori_loop(0, tg, start, 0)
    lax.fori_loop(0, tg, wait, 0)
    o_ref[...] = buf[...].astype(o_ref.dtype)


def _gather_rows(row_tok, x, tg):
    n_rows = row_tok.shape[0]
    d = x.shape[1]
    grid_spec = pltpu.PrefetchScalarGridSpec(
        num_scalar_prefetch=1,
        grid=(n_rows // tg,),
        in_specs=[pl.BlockSpec(memory_space=pl.ANY)],
        out_specs=pl.BlockSpec((tg, d), lambda i, tok: (i, 0)),
        scratch_shapes=[pltpu.VMEM((tg, d), x.dtype), pltpu.SemaphoreType.DMA(())],
    )
    return pl.pallas_call(
        functools.partial(_gather_kernel, tg=tg),
        out_shape=jax.ShapeDtypeStruct((n_rows, d), BF16),
        grid_spec=grid_spec,
        compiler_params=_cparams(("arbitrary",)),
        name="moe_gather",
    )(row_tok, x)


def _moe_ffn_kernel(te_ref, nu_ref, x_ref, wg_ref, wu_ref, wd_ref, y_ref):
    i = pl.program_id(0)
    j = pl.program_id(1)

    @pl.when(i < nu_ref[0])
    def _():
        part = _swiglu_partial(x_ref[...], wg_ref.at[0], wu_ref.at[0], wd_ref.at[0])

        @pl.when(j == 0)
        def _():
            y_ref[...] = part

        @pl.when(j > 0)
        def _():
            y_ref[...] += part

    @pl.when((i >= nu_ref[0]) & (j == 0))
    def _():
        y_ref[...] = jnp.zeros_like(y_ref)


def _moe_ffn(tile_expert, n_used, xs, wg, wu, wd, tm, tf):
    n_rows, d = xs.shape
    f = wg.shape[2]
    nf = f // tf

    def jj(i, j, nu):
        return jnp.where(i < nu[0], j, nf - 1)

    grid_spec = pltpu.PrefetchScalarGridSpec(
        num_scalar_prefetch=2,
        grid=(n_rows // tm, nf),
        in_specs=[pl.BlockSpec((tm, d), lambda i, j, te, nu: (i, 0)),
                  pl.BlockSpec((1, d, tf), lambda i, j, te, nu: (te[i], 0, jj(i, j, nu))),
                  pl.BlockSpec((1, d, tf), lambda i, j, te, nu: (te[i], 0, jj(i, j, nu))),
                  pl.BlockSpec((1, tf, d), lambda i, j, te, nu: (te[i], jj(i, j, nu), 0))],
        out_specs=pl.BlockSpec((tm, d), lambda i, j, te, nu: (i, 0)),
    )
    return pl.pallas_call(
        _moe_ffn_kernel,
        out_shape=jax.ShapeDtypeStruct((n_rows, d), F32),
        grid_spec=grid_spec,
        compiler_params=_cparams(("arbitrary", "arbitrary")),
        name="moe_ffn",
    )(tile_expert, n_used, xs, wg, wu, wd)


def _combine_kernel(dest_ref, y_hbm, h_ref, gate_ref, gpost_ref, o_ref, buf, sem, *, tc):
    base = pl.program_id(0) * tc

    def row_copy(r, k):
        return pltpu.make_async_copy(y_hbm.at[pl.ds(dest_ref[(base + r) * TOP_K + k], 1)],
                                     buf.at[k, pl.ds(r, 1)], sem)

    def start(r, c):
        row_copy(r, 0).start()
        row_copy(r, 1).start()
        return c

    def wait(r, c):
        row_copy(r, 0).wait()
        row_copy(r, 1).wait()
        return c

    lax.fori_loop(0, tc, start, 0)
    lax.fori_loop(0, tc, wait, 0)
    gate = gate_ref[...]
    y = buf[0] * gate[:, 0:1] + buf[1] * gate[:, 1:2]
    o_ref[...] = h_ref[...] + _rms(y, gpost_ref[...])


def _combine(dest_flat, ys, h, gates, gpost, s_real, tc):
    d = h.shape[1]
    grid_spec = pltpu.PrefetchScalarGridSpec(
        num_scalar_prefetch=1,
        grid=(s_real // tc,),
        in_specs=[pl.BlockSpec(memory_space=pl.ANY),
                  pl.BlockSpec((tc, d), lambda i, dest: (i, 0)),
                  pl.BlockSpec((tc, LANES), lambda i, dest: (i, 0)),
                  pl.BlockSpec((1, d), lambda i, dest: (0, 0))],
        out_specs=pl.BlockSpec((tc, d), lambda i, dest: (i, 0)),
        scratch_shapes=[pltpu.VMEM((TOP_K, tc, d), ys.dtype), pltpu.SemaphoreType.DMA(())],
    )
    return pl.pallas_call(
        functools.partial(_combine_kernel, tc=tc),
        out_shape=jax.ShapeDtypeStruct((s_real, d), F32),
        grid_spec=grid_spec,
        compiler_params=_cparams(("arbitrary",)),
        name="moe_combine",
    )(dest_flat, ys, h, gates, gpost)


def _rope_angles(pos, dim, theta):
    inv_freq = theta ** (-jnp.arange(0, dim, 2, dtype=F32) / dim)
    return pos.astype(F32)[:, None] * inv_freq[None, :]


def _tables(s_real, lp):
    r = jnp.arange(lp, dtype=jnp.int32)
    is_real = r < s_real
    p = r - s_real
    pos = jnp.where(is_real, r + N_META, jnp.clip(p, 0, N_META - 1))
    grid_row = jnp.where(is_real, r // GRID_W, -1)
    grid_col = jnp.where(is_real, r % GRID_W, jnp.clip(p, 0, N_META - 1))

    def cs(ang):
        c, s = jnp.cos(ang), jnp.sin(ang)
        return jnp.concatenate([c, c], axis=1), jnp.concatenate([-s, s], axis=1)

    a_c, a_s = cs(_rope_angles(pos, A_ROPE, ROPE_THETA))
    pad_c = jnp.ones((lp, LANES - A_ROPE), F32)
    acos = jnp.concatenate([a_c, pad_c], axis=1)
    asin = jnp.concatenate([a_s, 0.0 * pad_c], axis=1)
    r_c, r_s = cs(_rope_angles(grid_row, B_HEAD_DIM // 2, AXIAL_THETA))
    c_c, c_s = cs(_rope_angles(grid_col, B_HEAD_DIM // 2, AXIAL_THETA))
    bcos = jnp.concatenate([r_c, c_c], axis=1)
    bsin = jnp.concatenate([r_s, c_s], axis=1)
    w_c, w_s = cs(_rope_angles(pos, C_ROT, ROPE_THETA))
    one = jnp.ones((lp, C_HEAD_DIM - C_ROT), F32)
    ccos = jnp.concatenate([w_c, one, w_c, one], axis=1)
    csin = jnp.concatenate([w_s, 0.0 * one, w_s, 0.0 * one], axis=1)
    return acos, asin, bcos, bsin, ccos, csin


def _route(idx, tm, n_tiles):
    n_tok = idx.shape[0]
    e_flat = idx.reshape(-1)
    n_assign = e_flat.shape[0]
    t_flat = jnp.repeat(jnp.arange(n_tok, dtype=jnp.int32), TOP_K)
    order = jnp.argsort(e_flat * n_assign + jnp.arange(n_assign, dtype=jnp.int32))
    e_sorted, t_sorted = e_flat[order], t_flat[order]
    counts = jnp.sum((e_flat[:, None] == jnp.arange(N_EXPERTS, dtype=jnp.int32)[None, :]).astype(jnp.int32), axis=0)
    padded = (counts + tm - 1) // tm * tm
    start = jnp.cumsum(counts) - counts
    pend = jnp.cumsum(padded)
    pstart = pend - padded
    dest_sorted = pstart[e_sorted] + (jnp.arange(n_assign, dtype=jnp.int32) - start[e_sorted])
    row_tok = jnp.zeros((n_tiles * tm,), jnp.int32).at[dest_sorted].set(t_sorted)
    dest_flat = jnp.zeros((n_assign,), jnp.int32).at[order].set(dest_sorted)
    tile_expert = jnp.minimum(
        jnp.searchsorted(pend, jnp.arange(n_tiles, dtype=jnp.int32) * tm, side='right'), N_EXPERTS - 1).astype(jnp.int32)
    n_used = (pend[-1] // tm).astype(jnp.int32).reshape(1)
    return row_tok, dest_flat, tile_expert, n_used


def kernel(x, meta_tokens, even_attn_pre, even_attn_post, even_w_in, a_q_norm, a_wq_b, a_kv_norm, a_wkv_b, b_q_norm, b_k_norm, even_w_out, even_ffn_pre, even_ffn_post, ffn_w_gate, ffn_w_up, ffn_w_down, odd_attn_pre, odd_attn_post, c_w_qkv, c_b_qkv, c_sink, c_w_out, c_b_out, odd_ffn_pre, odd_ffn_post, moe_router, moe_w_gate, moe_w_up, moe_w_down):
    bsz, s_real, d = x.shape
    assert bsz == 1 and s_real % ROW_PAD == 0 and s_real % GRID_W == 0
    lp = s_real + ROW_PAD
    n_tok = s_real + N_META
    vec = lambda g: g.reshape(1, -1).astype(F32)
    acos, asin, bcos, bsin, ccos, csin = _tables(s_real, lp)
    tm_proj = _row_tile(lp, 256)

    h, xn = _prep(x[0], meta_tokens, vec(even_attn_pre[0]), lp)

    w_in = even_w_in[0]
    w_in_p = jnp.concatenate([w_in[:, :A_IN], jnp.zeros((d, LANES - A_ROPE), F32), w_in[:, A_IN:]], axis=1).astype(BF16)
    wq_p = jnp.pad(a_wq_b[0].reshape(A_Q_RANK, A_HEADS, A_NOPE + A_ROPE),
                   ((0, 0), (0, 0), (0, A_QK - A_NOPE - A_ROPE))).reshape(A_Q_RANK, A_HEADS * A_QK).astype(BF16)
    qa, kva, krot, qb, kb, vb = _even_proj(
        xn, w_in_p, vec(a_q_norm[0]), wq_p, vec(a_kv_norm[0]), a_wkv_b[0].astype(BF16),
        vec(b_q_norm[0]), vec(b_k_norm[0]), acos, asin, bcos, bsin, tm_proj)

    tk = _row_tile(s_real, 512)
    oa = _mla_attention(qa, kva, krot, s_real, _row_tile(lp, 1280), tk)
    ob = _gqa_attention(qb, kb, vb, s_real, _row_tile(lp, 256), tk)

    w_out = even_w_out[0].astype(BF16)
    n_a = A_HEADS * A_V
    h, xn = _out_proj0(oa, ob, w_out[:n_a], w_out[n_a:], h, vec(even_attn_post[0]), vec(even_ffn_pre[0]), tm_proj)

    tf = 512 if ffn_w_gate.shape[2] % 512 == 0 else LANES
    h, xn = _ffn0(xn, ffn_w_gate[0].astype(BF16), ffn_w_up[0].astype(BF16), ffn_w_down[0].astype(BF16),
                  h, vec(even_ffn_post[0]), vec(odd_attn_pre[0]), _row_tile(lp, 640), tf)

    nq = C_HEADS * C_HEAD_DIM
    nkv = C_KV_HEADS * C_HEAD_DIM

    def dup_heads(w):
        w4 = w.reshape(w.shape[:-1] + (C_KV_HEADS, C_HEAD_DIM))
        return jnp.concatenate([w4, w4], axis=-1).reshape(w.shape[:-1] + (C_KV_HEADS * LANES,))

    wc, bc = c_w_qkv[0], c_b_qkv[0]
    wc_p = jnp.concatenate([wc[:, :nq], dup_heads(wc[:, nq:nq + nkv]), dup_heads(wc[:, nq + nkv:])], axis=1).astype(BF16)
    bc_p = jnp.concatenate([bc[:nq], dup_heads(bc[nq:nq + nkv]), dup_heads(bc[nq + nkv:])]).reshape(1, -1)
    qc, kc, vc = _odd_proj(xn, wc_p, bc_p, ccos, csin, tm_proj)
    oc = _win_attention(qc, kc, vc, c_sink[0].astype(F32), s_real, ROW_PAD)

    router_p = jnp.pad(moe_router[0], ((0, 0), (0, LANES - N_EXPERTS)))
    h, xn_f32, idx_l, gate_l = _out_proj1(oc, c_w_out[0].astype(BF16), vec(c_b_out[0]), h, vec(odd_attn_post[0]),
                                          vec(odd_ffn_pre[0]), router_p, _row_tile(lp, 256))

    tm_moe = 1024 if s_real >= 8192 else 256
    n_assign = n_tok * TOP_K
    n_tiles = -(-(n_assign + N_EXPERTS * (tm_moe - 1)) // tm_moe)
    row_tok, dest_flat, tile_expert, n_used = _route(idx_l[:n_tok, :TOP_K], tm_moe, n_tiles)
    xs = _gather_rows(row_tok, xn_f32, 256)
    tf_moe = 256 if moe_w_gate.shape[3] % 256 == 0 else LANES
    ys = _moe_ffn(tile_expert, n_used, xs, moe_w_gate[0], moe_w_up[0], moe_w_down[0], tm_moe, tf_moe)
    out = _combine(dest_flat, ys, h, gate_l, vec(odd_ffn_post[0]), s_real, 256)
    return out[None]
```

```python
import functools
import math

import jax
import jax.numpy as jnp
from jax import lax
from jax.experimental import pallas as pl
from jax.experimental.pallas import tpu as pltpu

F32 = jnp.float32
BF16 = jnp.bfloat16

N_META = 16
GRID_W = 64
RMS_EPS = 1e-6
ROPE_THETA = 500000.0
AXIAL_THETA = 10000.0
A_HEADS, A_Q_RANK, A_KV_RANK, A_NOPE, A_ROPE, A_V = 8, 512, 256, 128, 64, 128
A_IN = A_Q_RANK + A_KV_RANK + A_ROPE
B_HEADS, B_KV_HEADS, B_HEAD_DIM = 8, 2, 128
C_HEADS, C_KV_HEADS, C_HEAD_DIM, C_WINDOW = 32, 4, 64, 128
C_ROT = C_HEAD_DIM // 4
N_EXPERTS, TOP_K = 8, 2

LANES = 128
V7X_VMEM_LIMIT = 56 * 1024 * 1024
LOG2E = math.log2(math.e)

ROW_PAD = 256
A_QK = 2 * LANES


def _cparams(sem):
    return pltpu.CompilerParams(dimension_semantics=sem, vmem_limit_bytes=V7X_VMEM_LIMIT)


def _row_tile(n_rows, target):
    best = LANES
    t = LANES
    while t <= min(n_rows, target):
        if n_rows % t == 0:
            best = t
        t += LANES
    return best


def _rms(x, g):
    ms = jnp.mean(x * x, axis=-1, keepdims=True)
    return x * lax.rsqrt(ms + RMS_EPS) * g


def _dot(a, b):
    return jnp.dot(a, b, preferred_element_type=F32)


def _dot_nt(a, b):
    return lax.dot_general(a, b, (((1,), (1,)), ((), ())), preferred_element_type=F32)


def _half_swap(x, half):
    lane = lax.broadcasted_iota(jnp.int32, x.shape, x.ndim - 1)
    up = pltpu.roll(x, x.shape[-1] - half, x.ndim - 1)
    down = pltpu.roll(x, half, x.ndim - 1)
    return jnp.where((lane & half) == 0, up, down)


def _prep_kernel(x_ref, meta_ref, g_ref, h_ref, xn_ref, *, n_real_blocks):
    i = pl.program_id(0)

    @pl.when(i < n_real_blocks)
    def _():
        h = x_ref[...]
        h_ref[...] = h
        xn_ref[...] = _rms(h, g_ref[...]).astype(xn_ref.dtype)

    @pl.when(i >= n_real_blocks)
    def _():
        h_ref[...] = jnp.zeros_like(h_ref)
        xn_ref[...] = jnp.zeros_like(xn_ref)
        m = meta_ref[...]
        h_ref[0:N_META, :] = m
        xn_ref[0:N_META, :] = _rms(m, g_ref[...]).astype(xn_ref.dtype)


def _prep(x2d, meta, g, lp):
    s, d = x2d.shape
    tm = ROW_PAD
    nrb = s // tm
    return pl.pallas_call(
        functools.partial(_prep_kernel, n_real_blocks=nrb),
        out_shape=(jax.ShapeDtypeStruct((lp, d), F32), jax.ShapeDtypeStruct((lp, d), BF16)),
        grid=(lp // tm,),
        in_specs=[pl.BlockSpec((tm, d), lambda i: (jnp.minimum(i, nrb - 1), 0)),
                  pl.BlockSpec((N_META, d), lambda i: (0, 0)),
                  pl.BlockSpec((1, d), lambda i: (0, 0))],
        out_specs=(pl.BlockSpec((tm, d), lambda i: (i, 0)), pl.BlockSpec((tm, d), lambda i: (i, 0))),
        compiler_params=_cparams(("parallel",)),
        name="prep",
    )(x2d, meta, g)


def _even_proj_kernel(xn_ref, w_in_ref, gq_ref, wq_ref, gkv_ref, wkv_ref, bgq_ref, bgk_ref,
                      acos_ref, asin_ref, bcos_ref, bsin_ref,
                      qaT_ref, kn_ref, vaT_ref, krot_ref, qbT_ref, kb_ref, vbT_ref):
    z = _dot(xn_ref[...], w_in_ref[...])
    acos, asin = acos_ref[...], asin_ref[...]
    bcos, bsin = bcos_ref[...], bsin_ref[...]
    sa = LOG2E / math.sqrt(A_NOPE + A_ROPE)
    sb = LOG2E / math.sqrt(B_HEAD_DIM)
    o = 0
    cqn = _rms(z[:, o:o + A_Q_RANK], gq_ref[...]).astype(BF16)
    o += A_Q_RANK
    qa = _dot(cqn, wq_ref[...])
    for h in range(A_HEADS):
        c = h * A_QK
        qaT_ref[c:c + LANES, :] = (qa[:, c:c + LANES] * sa).T.astype(BF16)
        rot = qa[:, c + LANES:c + A_QK]
        rot = rot * acos + _half_swap(rot, A_ROPE // 2) * asin
        qaT_ref[c + LANES:c + A_QK, :] = (rot * sa).T.astype(BF16)
    ckvn = _rms(z[:, o:o + A_KV_RANK], gkv_ref[...]).astype(BF16)
    o += A_KV_RANK
    kv = _dot(ckvn, wkv_ref[...])
    for h in range(A_HEADS):
        c = h * (A_NOPE + A_V)
        kn_ref[:, h * A_NOPE:(h + 1) * A_NOPE] = kv[:, c:c + A_NOPE].astype(BF16)
        vaT_ref[h * A_V:(h + 1) * A_V, :] = kv[:, c + A_NOPE:c + A_NOPE + A_V].T.astype(BF16)
    kr = z[:, o:o + LANES]
    o += LANES
    krot_ref[...] = (kr * acos + _half_swap(kr, A_ROPE // 2) * asin).astype(BF16)

    def norm_rope(x, g):
        xn = _rms(x, g)
        return xn * bcos + _half_swap(xn, B_HEAD_DIM // 4) * bsin

    for h in range(B_HEADS):
        c = h * B_HEAD_DIM
        qbT_ref[c:c + B_HEAD_DIM, :] = (norm_rope(z[:, o + c:o + c + B_HEAD_DIM], bgq_ref[...]) * sb).T.astype(BF16)
    o += B_HEADS * B_HEAD_DIM
    for h in range(B_KV_HEADS):
        c = h * B_HEAD_DIM
        kb_ref[:, c:c + B_HEAD_DIM] = norm_rope(z[:, o + c:o + c + B_HEAD_DIM], bgk_ref[...]).astype(BF16)
    o += B_KV_HEADS * B_HEAD_DIM
    for h in range(B_KV_HEADS):
        c = h * B_HEAD_DIM
        vbT_ref[c:c + B_HEAD_DIM, :] = z[:, o + c:o + c + B_HEAD_DIM].T.astype(BF16)


def _even_proj(xn, w_in_p, gq, wq_p, gkv, wkv, bgq, bgk, acos, asin, bcos, bsin, tm):
    lp, d = xn.shape
    row = lambda n: pl.BlockSpec((tm, n), lambda i: (i, 0))
    col = lambda n: pl.BlockSpec((n, tm), lambda i: (0, i))
    full = lambda a: pl.BlockSpec(a.shape, lambda i: (0, 0))
    outs = (("t", A_HEADS * A_QK), ("n", A_HEADS * A_NOPE), ("t", A_HEADS * A_V), ("n", LANES),
            ("t", B_HEADS * B_HEAD_DIM), ("n", B_KV_HEADS * B_HEAD_DIM), ("t", B_KV_HEADS * B_HEAD_DIM))
    return pl.pallas_call(
        _even_proj_kernel,
        out_shape=tuple(jax.ShapeDtypeStruct((n, lp) if k == "t" else (lp, n), BF16) for k, n in outs),
        grid=(lp // tm,),
        in_specs=[row(d), full(w_in_p), full(gq), full(wq_p), full(gkv), full(wkv), full(bgq), full(bgk),
                  row(LANES), row(LANES), row(LANES), row(LANES)],
        out_specs=tuple(col(n) if k == "t" else row(n) for k, n in outs),
        compiler_params=_cparams(("parallel",)),
        name="even_proj",
    )(xn, w_in_p, gq, wq_p, gkv, wkv, bgq, bgk, acos, asin, bcos, bsin)


def _flash_t(qT, k_chunk, vT_chunk, k_meta, vT_meta, m_sc, l_sc, acc_sc, s0_sc, s1_sc, *, n_chunks, tw):
    nq = qT.shape[1]
    subs = [slice(j, min(j + tw, nq)) for j in range(0, nq, tw)]
    km, vTm = k_meta(), vT_meta()
    for sl in subs:
        s = _dot(km, qT[:, sl])
        key = lax.broadcasted_iota(jnp.int32, s.shape, 0)
        s = jnp.where(key < N_META, s, -jnp.inf)
        m = jnp.max(s, axis=0, keepdims=True)
        p = jnp.exp2(s - m)
        m_sc[:, sl] = m
        l_sc[:, sl] = jnp.sum(p, axis=0, keepdims=True)
        acc_sc[:, sl] = _dot(vTm, p.astype(BF16))

    def softmax_pv(s_ref, c):
        vT = vT_chunk(c)
        for sl in subs:
            s = s_ref[:, sl]
            m_prev = m_sc[:, sl]
            m_new = jnp.maximum(m_prev, jnp.max(s, axis=0, keepdims=True))
            alpha = jnp.exp2(m_prev - m_new)
            p = jnp.exp2(s - m_new)
            l_sc[:, sl] = alpha * l_sc[:, sl] + jnp.sum(p, axis=0, keepdims=True)
            acc_sc[:, sl] = alpha * acc_sc[:, sl] + _dot(vT, p.astype(BF16))
            m_sc[:, sl] = m_new

    def step(i, carry):
        c = 2 * i
        s1_sc[...] = _dot(k_chunk(c + 1), qT)
        softmax_pv(s0_sc, c)
        s0_sc[...] = _dot(k_chunk(jnp.minimum(c + 2, n_chunks - 1)), qT)
        softmax_pv(s1_sc, c + 1)
        return carry

    assert n_chunks % 2 == 0
    s0_sc[...] = _dot(k_chunk(0), qT)
    lax.fori_loop(0, n_chunks // 2, step, 0)


def _mla_attn_kernel(qT_ref, kn_ref, kr_ref, vT_ref, knm_ref, krm_ref, vTm_ref, o_ref, m_sc, l_sc, acc_sc, s0_sc, s1_sc,
                     *, tk, tw):
    rows = lambda c: pl.ds(pl.multiple_of(c * tk, tk), tk)
    _flash_t(qT_ref[...],
             lambda c: jnp.concatenate([kn_ref[rows(c), :], kr_ref[rows(c), :]], axis=1),
             lambda c: vT_ref[:, rows(c)],
             lambda: jnp.concatenate([knm_ref[...], krm_ref[...]], axis=1),
             lambda: vTm_ref[...],
             m_sc, l_sc, acc_sc, s0_sc, s1_sc, n_chunks=kn_ref.shape[0] // tk, tw=tw)
    o_ref[...] = (acc_sc[...] / l_sc[...]).T.astype(o_ref.dtype)


def _mla_attention(qaT, kn, krot, vaT, s_real, tq, tk, tw):
    lp = qaT.shape[1]
    mb = s_real // LANES
    return pl.pallas_call(
        functools.partial(_mla_attn_kernel, tk=tk, tw=tw),
        out_shape=jax.ShapeDtypeStruct((lp, A_HEADS * A_V), BF16),
        grid=(A_HEADS, lp // tq),
        in_specs=[pl.BlockSpec((A_QK, tq), lambda h, i: (h, i)),
                  pl.BlockSpec((s_real, A_NOPE), lambda h, i: (0, h)),
                  pl.BlockSpec((s_real, LANES), lambda h, i: (0, 0)),
                  pl.BlockSpec((A_V, s_real), lambda h, i: (h, 0)),
                  pl.BlockSpec((LANES, A_NOPE), lambda h, i: (mb, h)),
                  pl.BlockSpec((LANES, LANES), lambda h, i: (mb, 0)),
                  pl.BlockSpec((A_V, LANES), lambda h, i: (h, mb))],
        out_specs=pl.BlockSpec((tq, A_V), lambda h, i: (i, h)),
        scratch_shapes=[pltpu.VMEM((1, tq), F32), pltpu.VMEM((1, tq), F32), pltpu.VMEM((A_V, tq), F32),
                        pltpu.VMEM((tk, tq), F32), pltpu.VMEM((tk, tq), F32)],
        compiler_params=_cparams(("parallel", "parallel")),
        name="mla_attention",
    )(qaT, kn, krot, vaT, kn, krot, vaT)


def _gqa_attn_kernel(qT_ref, k_ref, vT_ref, km_ref, vTm_ref, o_ref, m_sc, l_sc, acc_sc, s0_sc, s1_sc, *, tk, tw, n_stack):
    dh = k_ref.shape[1]
    tq = qT_ref.shape[1]
    rows = lambda c: pl.ds(pl.multiple_of(c * tk, tk), tk)
    qT = jnp.concatenate([qT_ref[j * dh:(j + 1) * dh, :] for j in range(n_stack)], axis=1)
    _flash_t(qT, lambda c: k_ref[rows(c), :], lambda c: vT_ref[:, rows(c)], lambda: km_ref[...], lambda: vTm_ref[...],
             m_sc, l_sc, acc_sc, s0_sc, s1_sc, n_chunks=k_ref.shape[0] // tk, tw=tw)
    for j in range(n_stack):
        sl = slice(j * tq, (j + 1) * tq)
        o_ref[:, j * dh:(j + 1) * dh] = (acc_sc[:, sl] / l_sc[:, sl]).T.astype(o_ref.dtype)


def _gqa_attention(qbT, kb, vbT, s_real, tq, tk, tw):
    lp = qbT.shape[1]
    grp = B_HEADS // B_KV_HEADS
    dh = B_HEAD_DIM
    mb = s_real // LANES
    return pl.pallas_call(
        functools.partial(_gqa_attn_kernel, tk=tk, tw=tw, n_stack=grp),
        out_shape=jax.ShapeDtypeStruct((lp, B_HEADS * dh), BF16),
        grid=(B_KV_HEADS, lp // tq),
        in_specs=[pl.BlockSpec((grp * dh, tq), lambda g, i: (g, i)),
                  pl.BlockSpec((s_real, dh), lambda g, i: (0, g)),
                  pl.BlockSpec((dh, s_real), lambda g, i: (g, 0)),
                  pl.BlockSpec((LANES, dh), lambda g, i: (mb, g)),
                  pl.BlockSpec((dh, LANES), lambda g, i: (g, mb))],
        out_specs=pl.BlockSpec((tq, grp * dh), lambda g, i: (i, g)),
        scratch_shapes=[pltpu.VMEM((1, grp * tq), F32), pltpu.VMEM((1, grp * tq), F32),
                        pltpu.VMEM((dh, grp * tq), F32),
                        pltpu.VMEM((tk, grp * tq), F32), pltpu.VMEM((tk, grp * tq), F32)],
        compiler_params=_cparams(("parallel", "parallel")),
        name="gqa_attention",
    )(qbT, kb, vbT, kb, vbT)


def _out_proj0_kernel(oa_ref, ob_ref, wa_ref, wb_ref, h_ref, gpost_ref, gnext_ref, hn_ref, xn_ref):
    y = _dot(oa_ref[...], wa_ref[...]) + _dot(ob_ref[...], wb_ref[...])
    h = h_ref[...] + _rms(y, gpost_ref[...])
    hn_ref[...] = h
    xn_ref[...] = _rms(h, gnext_ref[...]).astype(xn_ref.dtype)


def _out_proj0(oa, ob, wa, wb, h, gpost, gnext, tm):
    lp, d = h.shape
    row = lambda n: pl.BlockSpec((tm, n), lambda i: (i, 0))
    full = lambda a: pl.BlockSpec(a.shape, lambda i: (0, 0))
    return pl.pallas_call(
        _out_proj0_kernel,
        out_shape=(jax.ShapeDtypeStruct((lp, d), F32), jax.ShapeDtypeStruct((lp, d), BF16)),
        grid=(lp // tm,),
        in_specs=[row(oa.shape[1]), row(ob.shape[1]), full(wa), full(wb), row(d), full(gpost), full(gnext)],
        out_specs=(row(d), row(d)),
        compiler_params=_cparams(("parallel",)),
        name="out_proj0",
    )(oa, ob, wa, wb, h, gpost, gnext)


def _swiglu_partial(x, wg_ref, wu_ref, wd_ref):
    g = _dot(x, wg_ref[...].astype(BF16))
    u = _dot(x, wu_ref[...].astype(BF16))
    a = (g * jax.nn.sigmoid(g) * u).astype(BF16)
    return _dot(a, wd_ref[...].astype(BF16))


def _ffn0_kernel(x_ref, wg_ref, wu_ref, wd_ref, h_ref, gpost_ref, gnext_ref, hn_ref, xn_ref):
    j = pl.program_id(1)
    part = _swiglu_partial(x_ref[...], wg_ref, wu_ref, wd_ref)

    @pl.when(j == 0)
    def _():
        hn_ref[...] = part

    @pl.when(j > 0)
    def _():
        hn_ref[...] += part

    @pl.when(j == pl.num_programs(1) - 1)
    def _():
        h = h_ref[...] + _rms(hn_ref[...], gpost_ref[...])
        hn_ref[...] = h
        xn_ref[...] = _rms(h, gnext_ref[...]).astype(xn_ref.dtype)


def _ffn0(xn, wg, wu, wd, h, gpost, gnext, tm, tf):
    lp, d = h.shape
    f = wg.shape[1]
    row = pl.BlockSpec((tm, d), lambda i, j: (i, 0))
    vec = pl.BlockSpec((1, d), lambda i, j: (0, 0))
    return pl.pallas_call(
        _ffn0_kernel,
        out_shape=(jax.ShapeDtypeStruct((lp, d), F32), jax.ShapeDtypeStruct((lp, d), BF16)),
        grid=(lp // tm, f // tf),
        in_specs=[row, pl.BlockSpec((d, tf), lambda i, j: (0, j)), pl.BlockSpec((d, tf), lambda i, j: (0, j)),
                  pl.BlockSpec((tf, d), lambda i, j: (j, 0)), row, vec, vec],
        out_specs=(row, row),
        compiler_params=_cparams(("parallel", "arbitrary")),
        name="ffn0",
    )(xn, wg, wu, wd, h, gpost, gnext)


def _odd_proj_kernel(xn_ref, w_ref, b_ref, cos_ref, sin_ref, q_ref, k_ref, v_ref):
    z = _dot(xn_ref[...], w_ref[...]) + b_ref[...]
    cos, sin = cos_ref[...], sin_ref[...]
    sc = LOG2E / math.sqrt(C_HEAD_DIM)
    nq = q_ref.shape[1]
    nk = k_ref.shape[1]

    def rope(x):
        return x * cos + _half_swap(x, C_ROT // 2) * sin

    for c in range(0, nq, LANES):
        q_ref[:, c:c + LANES] = (rope(z[:, c:c + LANES]) * sc).astype(BF16)
    for c in range(0, nk, LANES):
        k_ref[:, c:c + LANES] = rope(z[:, nq + c:nq + c + LANES]).astype(BF16)
    v_ref[...] = z[:, nq + nk:].astype(BF16)


def _odd_proj(xn, w_p, b_p, cos, sin, tm):
    lp, d = xn.shape
    nq = C_HEADS * C_HEAD_DIM
    nk = C_KV_HEADS * LANES
    row = lambda n: pl.BlockSpec((tm, n), lambda i: (i, 0))
    full = lambda a: pl.BlockSpec(a.shape, lambda i: (0, 0))
    return pl.pallas_call(
        _odd_proj_kernel,
        out_shape=(jax.ShapeDtypeStruct((lp, nq), BF16), jax.ShapeDtypeStruct((lp, nk), BF16),
                   jax.ShapeDtypeStruct((lp, nk), BF16)),
        grid=(lp // tm,),
        in_specs=[row(d), full(w_p), full(b_p), row(LANES), row(LANES)],
        out_specs=(row(nq), row(nk), row(nk)),
        compiler_params=_cparams(("parallel",)),
        name="odd_proj",
    )(xn, w_p, b_p, cos, sin)


def _win_attn_kernel(sink_ref, q_ref, k0, k1, k2, k3, km, v0, v1, v2, v3, vm, o_ref, *, s_real, tq):
    g = pl.program_id(0)
    i = pl.program_id(1)
    is_meta = i == pl.num_programs(1) - 1
    nband = 4 * LANES
    kcat = jnp.concatenate([k0[...], k1[...], k2[...], k3[...], km[...]], axis=0)
    vcat = jnp.concatenate([v0[...], v1[...], v2[...], v3[...], vm[...]], axis=0)
    ncol = kcat.shape[0]
    a = lax.broadcasted_iota(jnp.int32, (tq, ncol), 0)
    c = lax.broadcasted_iota(jnp.int32, (tq, ncol), 1)
    kidx = jnp.where(is_meta, 0, i * tq - LANES) + c
    qidx = i * tq + a
    ninf = jnp.float32(-jnp.inf)
    b_real = jnp.where(jnp.abs(qidx - kidx) <= C_WINDOW, 0.0, ninf)
    b_real = jnp.where(kidx >= 0, b_real, ninf)
    b_real = jnp.where(kidx < s_real, b_real, ninf)
    b_meta_q = jnp.where(kidx <= (C_WINDOW - N_META) + a, 0.0, ninf)
    b_meta_k = jnp.where(c - nband < N_META, 0.0, ninf)
    bias = jnp.where(c < nband, jnp.where(is_meta, b_meta_q, b_real), b_meta_k)
    lane = lax.broadcasted_iota(jnp.int32, (tq, LANES), 1)
    heads_per_group = C_HEADS // C_KV_HEADS
    for pr in range(heads_per_group // 2):
        qp = q_ref[:, pr * LANES:(pr + 1) * LANES]
        halves = []
        for half in range(2):
            sink = sink_ref[g * heads_per_group + 2 * pr + half] * LOG2E
            qm = jnp.where((lane >= C_HEAD_DIM) == (half == 1), qp, jnp.zeros_like(qp))
            s = _dot_nt(qm, kcat) + bias
            m = jnp.maximum(jnp.max(s, axis=1, keepdims=True), sink)
            e = jnp.exp2(s - m)
            denom = jnp.sum(e, axis=1, keepdims=True) + jnp.exp2(sink - m)
            halves.append(_dot(e.astype(BF16), vcat) / denom)
        o_ref[:, pr * LANES:(pr + 1) * LANES] = jnp.where(lane < C_HEAD_DIM, halves[0], halves[1]).astype(o_ref.dtype)


def _win_attention(qc, kc, vc, sink, s_real, tq):
    lp = qc.shape[0]
    nq = lp // tq
    nkb = s_real // LANES
    per = tq // LANES
    gw = (C_HEADS // C_KV_HEADS) * C_HEAD_DIM

    def band(k):
        def index(g, i, sink_ref):
            b = jnp.where(i == nq - 1, k, jnp.clip(i * per - 1 + k, 0, nkb - 1))
            return (b, g)
        return pl.BlockSpec((LANES, LANES), index)

    meta = pl.BlockSpec((LANES, LANES), lambda g, i, sink_ref: (nkb, g))
    grid_spec = pltpu.PrefetchScalarGridSpec(
        num_scalar_prefetch=1,
        grid=(C_KV_HEADS, nq),
        in_specs=[pl.BlockSpec((tq, gw), lambda g, i, sink_ref: (i, g))]
                 + [band(k) for k in range(4)] + [meta] + [band(k) for k in range(4)] + [meta],
        out_specs=pl.BlockSpec((tq, gw), lambda g, i, sink_ref: (i, g)),
    )
    return pl.pallas_call(
        functools.partial(_win_attn_kernel, s_real=s_real, tq=tq),
        out_shape=jax.ShapeDtypeStruct((lp, C_HEADS * C_HEAD_DIM), BF16),
        grid_spec=grid_spec,
        compiler_params=_cparams(("parallel", "parallel")),
        name="win_attention",
    )(sink, qc, kc, kc, kc, kc, kc, vc, vc, vc, vc, vc)


def _out_proj1_kernel(o_ref, w_ref, b_ref, h_ref, gpost_ref, gnext_ref, router_ref, hn_ref, xn_ref, idx_ref, gate_ref):
    y = _dot(o_ref[...], w_ref[...]) + b_ref[...]
    h = h_ref[...] + _rms(y, gpost_ref[...])
    hn_ref[...] = h
    xn = _rms(h, gnext_ref[...])
    xn_ref[...] = xn
    logit = [jnp.sum(xn * router_ref[e:e + 1, :], axis=1, keepdims=True) for e in range(N_EXPERTS)]

    def top(cols):
        v = functools.reduce(jnp.maximum, cols)
        i = jnp.full_like(v, N_EXPERTS, dtype=jnp.int32)
        for e in reversed(range(N_EXPERTS)):
            i = jnp.where(cols[e] == v, e, i)
        return v, i

    v1, i1 = top(logit)
    v2, i2 = top([jnp.where(i1 == e, -jnp.inf, logit[e]) for e in range(N_EXPERTS)])
    e2 = jnp.exp(v2 - v1)
    g1 = 1.0 / (1.0 + e2)
    g2 = e2 / (1.0 + e2)
    lane = lax.broadcasted_iota(jnp.int32, idx_ref.shape, 1)
    idx_ref[...] = jnp.where(lane == 0, i1, jnp.where(lane == 1, i2, 0))
    gate_ref[...] = jnp.where(lane == 0, g1, jnp.where(lane == 1, g2, 0.0))


def _out_proj1(o, w, b, h, gpost, gnext, router_p, tm):
    lp, d = h.shape
    row = lambda n: pl.BlockSpec((tm, n), lambda i: (i, 0))
    full = lambda a: pl.BlockSpec(a.shape, lambda i: (0, 0))
    return pl.pallas_call(
        _out_proj1_kernel,
        out_shape=(jax.ShapeDtypeStruct((lp, d), F32), jax.ShapeDtypeStruct((lp, d), F32),
                   jax.ShapeDtypeStruct((lp, LANES), jnp.int32), jax.ShapeDtypeStruct((lp, LANES), F32)),
        grid=(lp // tm,),
        in_specs=[row(o.shape[1]), full(w), full(b), row(d), full(gpost), full(gnext), full(router_p)],
        out_specs=(row(d), row(d), row(LANES), row(LANES)),
        compiler_params=_cparams(("parallel",)),
        name="out_proj1",
    )(o, w, b, h, gpost, gnext, router_p)


def _gather_kernel(tok_ref, x_hbm, o_ref, buf, sem, *, tg):
    i = pl.program_id(0)
    n = pl.num_programs(0)

    def row_copy(step, slot, r):
        return pltpu.make_async_copy(x_hbm.at[pl.ds(tok_ref[step * tg + r], 1)], buf.at[slot, pl.ds(r, 1)],
                                     sem.at[slot])

    def start_all(step, slot):
        def body(r, c):
            row_copy(step, slot, r).start()
            return c
        lax.fori_loop(0, tg, body, 0, unroll=8)

    def wait_all(step, slot):
        def body(r, c):
            row_copy(step, slot, r).wait()
            return c
        lax.fori_loop(0, tg, body, 0, unroll=8)

    @pl.when(i == 0)
    def _():
        start_all(0, 0)

    for slot in range(2):
        @pl.when((i % 2 == slot) & (i + 1 < n))
        def _():
            start_all(i + 1, 1 - slot)

        @pl.when(i % 2 == slot)
        def _():
            wait_all(i, slot)
            o_ref[...] = buf[slot].astype(o_ref.dtype)


def _gather_rows(row_tok, x, tg):
    n_rows = row_tok.shape[0]
    d = x.shape[1]
    grid_spec = pltpu.PrefetchScalarGridSpec(
        num_scalar_prefetch=1,
        grid=(n_rows // tg,),
        in_specs=[pl.BlockSpec(memory_space=pl.ANY)],
        out_specs=pl.BlockSpec((tg, d), lambda i, tok: (i, 0)),
        scratch_shapes=[pltpu.VMEM((2, tg, d), x.dtype), pltpu.SemaphoreType.DMA((2,))],
    )
    return pl.pallas_call(
        functools.partial(_gather_kernel, tg=tg),
        out_shape=jax.ShapeDtypeStruct((n_rows, d), BF16),
        grid_spec=grid_spec,
        compiler_params=_cparams(("arbitrary",)),
        name="moe_gather",
    )(row_tok, x)


def _moe_ffn_kernel(te_ref, nu_ref, x_ref, wg_ref, wu_ref, wd_ref, y_ref):
    i = pl.program_id(0)
    j = pl.program_id(1)

    @pl.when(i < nu_ref[0])
    def _():
        part = _swiglu_partial(x_ref[...], wg_ref.at[0], wu_ref.at[0], wd_ref.at[0])

        @pl.when(j == 0)
        def _():
            y_ref[...] = part

        @pl.when(j > 0)
        def _():
            y_ref[...] += part

    @pl.when((i >= nu_ref[0]) & (j == 0))
    def _():
        y_ref[...] = jnp.zeros_like(y_ref)


def _moe_ffn(tile_expert, n_used, xs, wg, wu, wd, tm, tf):
    n_rows, d = xs.shape
    f = wg.shape[2]
    nf = f // tf

    def jj(i, j, nu):
        return jnp.where(i < nu[0], j, nf - 1)

    grid_spec = pltpu.PrefetchScalarGridSpec(
        num_scalar_prefetch=2,
        grid=(n_rows // tm, nf),
        in_specs=[pl.BlockSpec((tm, d), lambda i, j, te, nu: (i, 0)),
                  pl.BlockSpec((1, d, tf), lambda i, j, te, nu: (te[i], 0, jj(i, j, nu))),
                  pl.BlockSpec((1, d, tf), lambda i, j, te, nu: (te[i], 0, jj(i, j, nu))),
                  pl.BlockSpec((1, tf, d), lambda i, j, te, nu: (te[i], jj(i, j, nu), 0))],
        out_specs=pl.BlockSpec((tm, d), lambda i, j, te, nu: (i, 0)),
    )
    return pl.pallas_call(
        _moe_ffn_kernel,
        out_shape=jax.ShapeDtypeStruct((n_rows, d), F32),
        grid_spec=grid_spec,
        compiler_params=_cparams(("arbitrary", "arbitrary")),
        name="moe_ffn",
    )(tile_expert, n_used, xs, wg, wu, wd)


def _combine_kernel(dest_ref, y_hbm, h_ref, gate_ref, gpost_ref, o_ref, buf, sem, *, tc):
    base = pl.program_id(0) * tc

    def row_copy(r, k):
        return pltpu.make_async_copy(y_hbm.at[pl.ds(dest_ref[(base + r) * TOP_K + k], 1)],
                                     buf.at[k, pl.ds(r, 1)], sem)

    def start(r, c):
        row_copy(r, 0).start()
        row_copy(r, 1).start()
        return c

    def wait(r, c):
        row_copy(r, 0).wait()
        row_copy(r, 1).wait()
        return c

    lax.fori_loop(0, tc, start, 0)
    lax.fori_loop(0, tc, wait, 0)
    gate = gate_ref[...]
    y = buf[0] * gate[:, 0:1] + buf[1] * gate[:, 1:2]
    o_ref[...] = h_ref[...] + _rms(y, gpost_ref[...])


def _combine(dest_flat, ys, h, gates, gpost, s_real, tc):
    d = h.shape[1]
    grid_spec = pltpu.PrefetchScalarGridSpec(
        num_scalar_prefetch=1,
        grid=(s_real // tc,),
        in_specs=[pl.BlockSpec(memory_space=pl.ANY),
                  pl.BlockSpec((tc, d), lambda i, dest: (i, 0)),
                  pl.BlockSpec((tc, LANES), lambda i, dest: (i, 0)),
                  pl.BlockSpec((1, d), lambda i, dest: (0, 0))],
        out_specs=pl.BlockSpec((tc, d), lambda i, dest: (i, 0)),
        scratch_shapes=[pltpu.VMEM((TOP_K, tc, d), ys.dtype), pltpu.SemaphoreType.DMA(())],
    )
    return pl.pallas_call(
        functools.partial(_combine_kernel, tc=tc),
        out_shape=jax.ShapeDtypeStruct((s_real, d), F32),
        grid_spec=grid_spec,
        compiler_params=_cparams(("arbitrary",)),
        name="moe_combine",
    )(dest_flat, ys, h, gates, gpost)


def _rope_angles(pos, dim, theta):
    inv_freq = theta ** (-jnp.arange(0, dim, 2, dtype=F32) / dim)
    return pos.astype(F32)[:, None] * inv_freq[None, :]


def _tables(s_real, lp):
    r = jnp.arange(lp, dtype=jnp.int32)
    is_real = r < s_real
    p = r - s_real
    pos = jnp.where(is_real, r + N_META, jnp.clip(p, 0, N_META - 1))
    grid_row = jnp.where(is_real, r // GRID_W, -1)
    grid_col = jnp.where(is_real, r % GRID_W, jnp.clip(p, 0, N_META - 1))

    def cs(ang):
        c, s = jnp.cos(ang), jnp.sin(ang)
        return jnp.concatenate([c, c], axis=1), jnp.concatenate([-s, s], axis=1)

    a_c, a_s = cs(_rope_angles(pos, A_ROPE, ROPE_THETA))
    pad_c = jnp.ones((lp, LANES - A_ROPE), F32)
    acos = jnp.concatenate([a_c, pad_c], axis=1)
    asin = jnp.concatenate([a_s, 0.0 * pad_c], axis=1)
    r_c, r_s = cs(_rope_angles(grid_row, B_HEAD_DIM // 2, AXIAL_THETA))
    c_c, c_s = cs(_rope_angles(grid_col, B_HEAD_DIM // 2, AXIAL_THETA))
    bcos = jnp.concatenate([r_c, c_c], axis=1)
    bsin = jnp.concatenate([r_s, c_s], axis=1)
    w_c, w_s = cs(_rope_angles(pos, C_ROT, ROPE_THETA))
    one = jnp.ones((lp, C_HEAD_DIM - C_ROT), F32)
    ccos = jnp.concatenate([w_c, one, w_c, one], axis=1)
    csin = jnp.concatenate([w_s, 0.0 * one, w_s, 0.0 * one], axis=1)
    return acos, asin, bcos, bsin, ccos, csin


def _route(idx, tm, n_tiles):
    n_tok = idx.shape[0]
    e_flat = idx.reshape(-1)
    n_assign = e_flat.shape[0]
    t_flat = jnp.repeat(jnp.arange(n_tok, dtype=jnp.int32), TOP_K)
    order = jnp.argsort(e_flat * n_assign + jnp.arange(n_assign, dtype=jnp.int32))
    e_sorted, t_sorted = e_flat[order], t_flat[order]
    counts = jnp.sum((e_flat[:, None] == jnp.arange(N_EXPERTS, dtype=jnp.int32)[None, :]).astype(jnp.int32), axis=0)
    padded = (counts + tm - 1) // tm * tm
    start = jnp.cumsum(counts) - counts
    pend = jnp.cumsum(padded)
    pstart = pend - padded
    dest_sorted = pstart[e_sorted] + (jnp.arange(n_assign, dtype=jnp.int32) - start[e_sorted])
    row_tok = jnp.zeros((n_tiles * tm,), jnp.int32).at[dest_sorted].set(t_sorted)
    dest_flat = jnp.zeros((n_assign,), jnp.int32).at[order].set(dest_sorted)
    tile_expert = jnp.minimum(
        jnp.searchsorted(pend, jnp.arange(n_tiles, dtype=jnp.int32) * tm, side='right'), N_EXPERTS - 1).astype(jnp.int32)
    n_used = (pend[-1] // tm).astype(jnp.int32).reshape(1)
    return row_tok, dest_flat, tile_expert, n_used


def kernel(x, meta_tokens, even_attn_pre, even_attn_post, even_w_in, a_q_norm, a_wq_b, a_kv_norm, a_wkv_b, b_q_norm, b_k_norm, even_w_out, even_ffn_pre, even_ffn_post, ffn_w_gate, ffn_w_up, ffn_w_down, odd_attn_pre, odd_attn_post, c_w_qkv, c_b_qkv, c_sink, c_w_out, c_b_out, odd_ffn_pre, odd_ffn_post, moe_router, moe_w_gate, moe_w_up, moe_w_down):
    bsz, s_real, d = x.shape
    assert bsz == 1 and s_real % ROW_PAD == 0 and s_real % GRID_W == 0
    lp = s_real + ROW_PAD
    n_tok = s_real + N_META
    vec = lambda g: g.reshape(1, -1).astype(F32)
    acos, asin, bcos, bsin, ccos, csin = _tables(s_real, lp)
    tm_proj = _row_tile(lp, 256)

    h, xn = _prep(x[0], meta_tokens, vec(even_attn_pre[0]), lp)

    w_in = even_w_in[0]
    w_in_p = jnp.concatenate([w_in[:, :A_IN], jnp.zeros((d, LANES - A_ROPE), F32), w_in[:, A_IN:]], axis=1).astype(BF16)
    wq_p = jnp.pad(a_wq_b[0].reshape(A_Q_RANK, A_HEADS, A_NOPE + A_ROPE),
                   ((0, 0), (0, 0), (0, A_QK - A_NOPE - A_ROPE))).reshape(A_Q_RANK, A_HEADS * A_QK).astype(BF16)
    qaT, kn, vaT, krot, qbT, kb, vbT = _even_proj(
        xn, w_in_p, vec(a_q_norm[0]), wq_p, vec(a_kv_norm[0]), a_wkv_b[0].astype(BF16),
        vec(b_q_norm[0]), vec(b_k_norm[0]), acos, asin, bcos, bsin, tm_proj)

    tk = _row_tile(s_real, 512)
    tq_a, tq_b = _row_tile(lp, 1280), _row_tile(lp, 256)
    oa = _mla_attention(qaT, kn, krot, vaT, s_real, tq_a, tk, tq_a)
    ob = _gqa_attention(qbT, kb, vbT, s_real, tq_b, tk, (B_HEADS // B_KV_HEADS) * tq_b)

    w_out = even_w_out[0].astype(BF16)
    n_a = A_HEADS * A_V
    h, xn = _out_proj0(oa, ob, w_out[:n_a], w_out[n_a:], h, vec(even_attn_post[0]), vec(even_ffn_pre[0]), tm_proj)

    tf = 512 if ffn_w_gate.shape[2] % 512 == 0 else LANES
    h, xn = _ffn0(xn, ffn_w_gate[0].astype(BF16), ffn_w_up[0].astype(BF16), ffn_w_down[0].astype(BF16),
                  h, vec(even_ffn_post[0]), vec(odd_attn_pre[0]), _row_tile(lp, 640), tf)

    nq = C_HEADS * C_HEAD_DIM
    nkv = C_KV_HEADS * C_HEAD_DIM

    def dup_heads(w):
        w4 = w.reshape(w.shape[:-1] + (C_KV_HEADS, C_HEAD_DIM))
        return jnp.concatenate([w4, w4], axis=-1).reshape(w.shape[:-1] + (C_KV_HEADS * LANES,))

    wc, bc = c_w_qkv[0], c_b_qkv[0]
    wc_p = jnp.concatenate([wc[:, :nq], dup_heads(wc[:, nq:nq + nkv]), dup_heads(wc[:, nq + nkv:])], axis=1).astype(BF16)
    bc_p = jnp.concatenate([bc[:nq], dup_heads(bc[nq:nq + nkv]), dup_heads(bc[nq + nkv:])]).reshape(1, -1)
    qc, kc, vc = _odd_proj(xn, wc_p, bc_p, ccos, csin, tm_proj)
    oc = _win_attention(qc, kc, vc, c_sink[0].astype(F32), s_real, ROW_PAD)

    h, xn_f32, idx_l, gate_l = _out_proj1(oc, c_w_out[0].astype(BF16), vec(c_b_out[0]), h, vec(odd_attn_post[0]),
                                          vec(odd_ffn_pre[0]), moe_router[0].T, _row_tile(lp, 256))

    tm_moe = 1024 if s_real >= 8192 else 256
    n_assign = n_tok * TOP_K
    n_tiles = -(-(n_assign + N_EXPERTS * (tm_moe - 1)) // tm_moe)
    row_tok, dest_flat, tile_expert, n_used = _route(idx_l[:n_tok, :TOP_K], tm_moe, n_tiles)
    xs = _gather_rows(row_tok, xn_f32, 256)
    tf_moe = 256 if moe_w_gate.shape[3] % 256 == 0 else LANES
    ys = _moe_ffn(tile_expert, n_used, xs, moe_w_gate[0], moe_w_up[0], moe_w_down[0], tm_moe, tf_moe)
    out = _combine(dest_flat, ys, h, gate_l, vec(odd_ffn_post[0]), s_real, 256)
    return out[None]
```

```python
import functools
import math

import jax
import jax.numpy as jnp
from jax import lax
from jax.experimental import pallas as pl
from jax.experimental.pallas import tpu as pltpu

F32 = jnp.float32
BF16 = jnp.bfloat16

N_META = 16
GRID_W = 64
RMS_EPS = 1e-6
ROPE_THETA = 500000.0
AXIAL_THETA = 10000.0
A_HEADS, A_Q_RANK, A_KV_RANK, A_NOPE, A_ROPE, A_V = 8, 512, 256, 128, 64, 128
A_IN = A_Q_RANK + A_KV_RANK + A_ROPE
B_HEADS, B_KV_HEADS, B_HEAD_DIM = 8, 2, 128
C_HEADS, C_KV_HEADS, C_HEAD_DIM, C_WINDOW = 32, 4, 64, 128
C_ROT = C_HEAD_DIM // 4
N_EXPERTS, TOP_K = 8, 2

LANES = 128
V7X_VMEM_LIMIT = 56 * 1024 * 1024
LOG2E = math.log2(math.e)

ROW_PAD = 256
A_QK = 2 * LANES


def _cparams(sem):
    return pltpu.CompilerParams(dimension_semantics=sem, vmem_limit_bytes=V7X_VMEM_LIMIT)


def _row_tile(n_rows, target):
    best = LANES
    t = LANES
    while t <= min(n_rows, target):
        if n_rows % t == 0:
            best = t
        t += LANES
    return best


def _rms(x, g):
    ms = jnp.mean(x * x, axis=-1, keepdims=True)
    return x * lax.rsqrt(ms + RMS_EPS) * g


def _dot(a, b):
    return jnp.dot(a, b, preferred_element_type=F32)


def _dot_nt(a, b):
    return lax.dot_general(a, b, (((1,), (1,)), ((), ())), preferred_element_type=F32)


def _half_swap(x, half):
    lane = lax.broadcasted_iota(jnp.int32, x.shape, x.ndim - 1)
    up = pltpu.roll(x, x.shape[-1] - half, x.ndim - 1)
    down = pltpu.roll(x, half, x.ndim - 1)
    return jnp.where((lane & half) == 0, up, down)


def _prep_kernel(x_ref, meta_ref, g_ref, h_ref, xn_ref, *, n_real_blocks):
    i = pl.program_id(0)

    @pl.when(i < n_real_blocks)
    def _():
        h = x_ref[...]
        h_ref[...] = h
        xn_ref[...] = _rms(h, g_ref[...]).astype(xn_ref.dtype)

    @pl.when(i >= n_real_blocks)
    def _():
        h_ref[...] = jnp.zeros_like(h_ref)
        xn_ref[...] = jnp.zeros_like(xn_ref)
        m = meta_ref[...]
        h_ref[0:N_META, :] = m
        xn_ref[0:N_META, :] = _rms(m, g_ref[...]).astype(xn_ref.dtype)


def _prep(x2d, meta, g, lp):
    s, d = x2d.shape
    tm = ROW_PAD
    nrb = s // tm
    return pl.pallas_call(
        functools.partial(_prep_kernel, n_real_blocks=nrb),
        out_shape=(jax.ShapeDtypeStruct((lp, d), F32), jax.ShapeDtypeStruct((lp, d), BF16)),
        grid=(lp // tm,),
        in_specs=[pl.BlockSpec((tm, d), lambda i: (jnp.minimum(i, nrb - 1), 0)),
                  pl.BlockSpec((N_META, d), lambda i: (0, 0)),
                  pl.BlockSpec((1, d), lambda i: (0, 0))],
        out_specs=(pl.BlockSpec((tm, d), lambda i: (i, 0)), pl.BlockSpec((tm, d), lambda i: (i, 0))),
        compiler_params=_cparams(("parallel",)),
        name="prep",
    )(x2d, meta, g)


def _even_proj_kernel(xn_ref, w_in_ref, gq_ref, wq_ref, gkv_ref, wkv_ref, bgq_ref, bgk_ref,
                      acos_ref, asin_ref, bcos_ref, bsin_ref,
                      qaT_ref, kn_ref, vaT_ref, krot_ref, qbT_ref, kb_ref, vbT_ref):
    z = _dot(xn_ref[...], w_in_ref[...])
    acos, asin = acos_ref[...], asin_ref[...]
    bcos, bsin = bcos_ref[...], bsin_ref[...]
    sa = LOG2E / math.sqrt(A_NOPE + A_ROPE)
    sb = LOG2E / math.sqrt(B_HEAD_DIM)
    o = 0
    cqn = _rms(z[:, o:o + A_Q_RANK], gq_ref[...]).astype(BF16)
    o += A_Q_RANK
    qa = _dot(cqn, wq_ref[...])
    for h in range(A_HEADS):
        c = h * A_QK
        qaT_ref[c:c + LANES, :] = (qa[:, c:c + LANES] * sa).T.astype(BF16)
        rot = qa[:, c + LANES:c + A_QK]
        rot = rot * acos + _half_swap(rot, A_ROPE // 2) * asin
        qaT_ref[c + LANES:c + A_QK, :] = (rot * sa).T.astype(BF16)
    ckvn = _rms(z[:, o:o + A_KV_RANK], gkv_ref[...]).astype(BF16)
    o += A_KV_RANK
    kv = _dot(ckvn, wkv_ref[...])
    for h in range(A_HEADS):
        c = h * (A_NOPE + A_V)
        kn_ref[:, h * A_NOPE:(h + 1) * A_NOPE] = kv[:, c:c + A_NOPE].astype(BF16)
        vaT_ref[h * A_V:(h + 1) * A_V, :] = kv[:, c + A_NOPE:c + A_NOPE + A_V].T.astype(BF16)
    kr = z[:, o:o + LANES]
    o += LANES
    krot_ref[...] = (kr * acos + _half_swap(kr, A_ROPE // 2) * asin).astype(BF16)

    def norm_rope(x, g):
        xn = _rms(x, g)
        return xn * bcos + _half_swap(xn, B_HEAD_DIM // 4) * bsin

    for h in range(B_HEADS):
        c = h * B_HEAD_DIM
        qbT_ref[c:c + B_HEAD_DIM, :] = (norm_rope(z[:, o + c:o + c + B_HEAD_DIM], bgq_ref[...]) * sb).T.astype(BF16)
    o += B_HEADS * B_HEAD_DIM
    for h in range(B_KV_HEADS):
        c = h * B_HEAD_DIM
        kb_ref[:, c:c + B_HEAD_DIM] = norm_rope(z[:, o + c:o + c + B_HEAD_DIM], bgk_ref[...]).astype(BF16)
    o += B_KV_HEADS * B_HEAD_DIM
    for h in range(B_KV_HEADS):
        c = h * B_HEAD_DIM
        vbT_ref[c:c + B_HEAD_DIM, :] = z[:, o + c:o + c + B_HEAD_DIM].T.astype(BF16)


def _even_proj(xn, w_in_p, gq, wq_p, gkv, wkv, bgq, bgk, acos, asin, bcos, bsin, tm):
    lp, d = xn.shape
    row = lambda n: pl.BlockSpec((tm, n), lambda i: (i, 0))
    col = lambda n: pl.BlockSpec((n, tm), lambda i: (0, i))
    full = lambda a: pl.BlockSpec(a.shape, lambda i: (0, 0))
    outs = (("t", A_HEADS * A_QK), ("n", A_HEADS * A_NOPE), ("t", A_HEADS * A_V), ("n", LANES),
            ("t", B_HEADS * B_HEAD_DIM), ("n", B_KV_HEADS * B_HEAD_DIM), ("t", B_KV_HEADS * B_HEAD_DIM))
    return pl.pallas_call(
        _even_proj_kernel,
        out_shape=tuple(jax.ShapeDtypeStruct((n, lp) if k == "t" else (lp, n), BF16) for k, n in outs),
        grid=(lp // tm,),
        in_specs=[row(d), full(w_in_p), full(gq), full(wq_p), full(gkv), full(wkv), full(bgq), full(bgk),
                  row(LANES), row(LANES), row(LANES), row(LANES)],
        out_specs=tuple(col(n) if k == "t" else row(n) for k, n in outs),
        compiler_params=_cparams(("parallel",)),
        name="even_proj",
    )(xn, w_in_p, gq, wq_p, gkv, wkv, bgq, bgk, acos, asin, bcos, bsin)


def _flash_t(qT, k_chunk, vT_chunk, k_meta, vT_meta, m_sc, l_sc, acc_sc, s_sc, p_sc, mx_sc, al_sc, *, n_chunks):
    km, vTm = k_meta(), vT_meta()
    s = _dot(km, qT)
    key = lax.broadcasted_iota(jnp.int32, s.shape, 0)
    s = jnp.where(key < N_META, s, -jnp.inf)
    m = jnp.max(s, axis=0, keepdims=True)
    p = jnp.exp2(s - m)
    m_sc[...] = m
    l_sc[...] = jnp.sum(p, axis=0, keepdims=True)
    acc_sc[...] = _dot(vTm, p.astype(BF16))

    nq = qT.shape[1]

    def scores(c, b):
        s = _dot(k_chunk(c), qT)
        s_sc[b, :, :nq] = s
        mx_sc[b] = jnp.max(s, axis=0, keepdims=True)

    def probs(b):
        m_prev = m_sc[...]
        m_new = jnp.maximum(m_prev, mx_sc[b])
        alpha = jnp.exp2(m_prev - m_new)
        p = jnp.exp2(s_sc[b, :, :nq] - m_new)
        l_sc[...] = alpha * l_sc[...] + jnp.sum(p, axis=0, keepdims=True)
        m_sc[...] = m_new
        al_sc[b] = alpha
        p_sc[b, :, :nq] = p.astype(BF16)

    def accumulate(c, b):
        acc_sc[...] = al_sc[b] * acc_sc[...] + _dot(vT_chunk(c), p_sc[b, :, :nq])

    assert n_chunks % 2 == 0
    scores(0, 0)
    scores(1, 1)
    probs(0)

    def step(i, carry):
        c = 2 * i
        scores(c + 2, 0)
        probs(1)
        accumulate(c, 0)
        scores(c + 3, 1)
        probs(0)
        accumulate(c + 1, 1)
        return carry

    lax.fori_loop(0, n_chunks // 2 - 1, step, 0)
    probs(1)
    accumulate(n_chunks - 2, 0)
    accumulate(n_chunks - 1, 1)


def _flash_scratch(dv, nq, tk):
    wide = nq if (nq // LANES) % 2 else nq + LANES
    return [pltpu.VMEM((1, nq), F32), pltpu.VMEM((1, nq), F32), pltpu.VMEM((dv, nq), F32),
            pltpu.VMEM((2, tk, wide), F32), pltpu.VMEM((2, tk, wide), BF16),
            pltpu.VMEM((2, 1, nq), F32), pltpu.VMEM((2, 1, nq), F32)]


def _key_chunk(s_real, target):
    best = None
    for t in range(LANES, min(s_real // 2, target) + 1, LANES):
        if s_real % (2 * t) == 0:
            best = t
    assert best is not None
    return best


def _mla_attn_kernel(qT_ref, kn_ref, kr_ref, vT_ref, knm_ref, krm_ref, vTm_ref, o_ref, *scratch, tk):
    rows = lambda c: pl.ds(pl.multiple_of(c * tk, tk), tk)
    _flash_t(qT_ref[...],
             lambda c: jnp.concatenate([kn_ref[rows(c), :], kr_ref[rows(c), :]], axis=1),
             lambda c: vT_ref[:, rows(c)],
             lambda: jnp.concatenate([knm_ref[...], krm_ref[...]], axis=1),
             lambda: vTm_ref[...],
             *scratch, n_chunks=kn_ref.shape[0] // tk)
    l_sc, acc_sc = scratch[1], scratch[2]
    o_ref[...] = (acc_sc[...] / l_sc[...]).T.astype(o_ref.dtype)


def _mla_attention(qaT, kn, krot, vaT, s_real, tq, tk):
    lp = qaT.shape[1]
    mb = s_real // LANES
    return pl.pallas_call(
        functools.partial(_mla_attn_kernel, tk=tk),
        out_shape=jax.ShapeDtypeStruct((lp, A_HEADS * A_V), BF16),
        grid=(A_HEADS, lp // tq),
        in_specs=[pl.BlockSpec((A_QK, tq), lambda h, i: (h, i)),
                  pl.BlockSpec((s_real, A_NOPE), lambda h, i: (0, h)),
                  pl.BlockSpec((s_real, LANES), lambda h, i: (0, 0)),
                  pl.BlockSpec((A_V, s_real), lambda h, i: (h, 0)),
                  pl.BlockSpec((LANES, A_NOPE), lambda h, i: (mb, h)),
                  pl.BlockSpec((LANES, LANES), lambda h, i: (mb, 0)),
                  pl.BlockSpec((A_V, LANES), lambda h, i: (h, mb))],
        out_specs=pl.BlockSpec((tq, A_V), lambda h, i: (i, h)),
        scratch_shapes=_flash_scratch(A_V, tq, tk),
        compiler_params=_cparams(("parallel", "parallel")),
        name="mla_attention",
    )(qaT, kn, krot, vaT, kn, krot, vaT)


def _gqa_attn_kernel(qT_ref, k_ref, vT_ref, km_ref, vTm_ref, o_ref, *scratch, tk, n_stack):
    dh = k_ref.shape[1]
    tq = qT_ref.shape[1]
    rows = lambda c: pl.ds(pl.multiple_of(c * tk, tk), tk)
    qT = jnp.concatenate([qT_ref[j * dh:(j + 1) * dh, :] for j in range(n_stack)], axis=1)
    _flash_t(qT, lambda c: k_ref[rows(c), :], lambda c: vT_ref[:, rows(c)], lambda: km_ref[...], lambda: vTm_ref[...],
             *scratch, n_chunks=k_ref.shape[0] // tk)
    l_sc, acc_sc = scratch[1], scratch[2]
    for j in range(n_stack):
        sl = slice(j * tq, (j + 1) * tq)
        o_ref[:, j * dh:(j + 1) * dh] = (acc_sc[:, sl] / l_sc[:, sl]).T.astype(o_ref.dtype)


def _gqa_attention(qbT, kb, vbT, s_real, tq, tk):
    lp = qbT.shape[1]
    grp = B_HEADS // B_KV_HEADS
    dh = B_HEAD_DIM
    mb = s_real // LANES
    return pl.pallas_call(
        functools.partial(_gqa_attn_kernel, tk=tk, n_stack=grp),
        out_shape=jax.ShapeDtypeStruct((lp, B_HEADS * dh), BF16),
        grid=(B_KV_HEADS, lp // tq),
        in_specs=[pl.BlockSpec((grp * dh, tq), lambda g, i: (g, i)),
                  pl.BlockSpec((s_real, dh), lambda g, i: (0, g)),
                  pl.BlockSpec((dh, s_real), lambda g, i: (g, 0)),
                  pl.BlockSpec((LANES, dh), lambda g, i: (mb, g)),
                  pl.BlockSpec((dh, LANES), lambda g, i: (g, mb))],
        out_specs=pl.BlockSpec((tq, grp * dh), lambda g, i: (i, g)),
        scratch_shapes=_flash_scratch(dh, grp * tq, tk),
        compiler_params=_cparams(("parallel", "parallel")),
        name="gqa_attention",
    )(qbT, kb, vbT, kb, vbT)


def _out_proj0_kernel(oa_ref, ob_ref, wa_ref, wb_ref, h_ref, gpost_ref, gnext_ref, hn_ref, xn_ref):
    y = _dot(oa_ref[...], wa_ref[...]) + _dot(ob_ref[...], wb_ref[...])
    h = h_ref[...] + _rms(y, gpost_ref[...])
    hn_ref[...] = h
    xn_ref[...] = _rms(h, gnext_ref[...]).astype(xn_ref.dtype)


def _out_proj0(oa, ob, wa, wb, h, gpost, gnext, tm):
    lp, d = h.shape
    row = lambda n: pl.BlockSpec((tm, n), lambda i: (i, 0))
    full = lambda a: pl.BlockSpec(a.shape, lambda i: (0, 0))
    return pl.pallas_call(
        _out_proj0_kernel,
        out_shape=(jax.ShapeDtypeStruct((lp, d), F32), jax.ShapeDtypeStruct((lp, d), BF16)),
        grid=(lp // tm,),
        in_specs=[row(oa.shape[1]), row(ob.shape[1]), full(wa), full(wb), row(d), full(gpost), full(gnext)],
        out_specs=(row(d), row(d)),
        compiler_params=_cparams(("parallel",)),
        name="out_proj0",
    )(oa, ob, wa, wb, h, gpost, gnext)


def _swiglu_partial(x, wg_ref, wu_ref, wd_ref):
    g = _dot(x, wg_ref[...].astype(BF16))
    u = _dot(x, wu_ref[...].astype(BF16))
    a = (g * jax.nn.sigmoid(g) * u).astype(BF16)
    return _dot(a, wd_ref[...].astype(BF16))


def _ffn0_kernel(x_ref, wg_ref, wu_ref, wd_ref, h_ref, gpost_ref, gnext_ref, hn_ref, xn_ref):
    j = pl.program_id(1)

    @pl.when(j == 0)
    def _():
        hn_ref[...] = jnp.zeros_like(hn_ref)

    hn_ref[...] += _swiglu_partial(x_ref[...], wg_ref, wu_ref, wd_ref)

    @pl.when(j == pl.num_programs(1) - 1)
    def _():
        h = h_ref[...] + _rms(hn_ref[...], gpost_ref[...])
        hn_ref[...] = h
        xn_ref[...] = _rms(h, gnext_ref[...]).astype(xn_ref.dtype)


def _ffn0(xn, wg, wu, wd, h, gpost, gnext, tm, tf):
    lp, d = h.shape
    f = wg.shape[1]
    row = pl.BlockSpec((tm, d), lambda i, j: (i, 0))
    vec = pl.BlockSpec((1, d), lambda i, j: (0, 0))
    return pl.pallas_call(
        _ffn0_kernel,
        out_shape=(jax.ShapeDtypeStruct((lp, d), F32), jax.ShapeDtypeStruct((lp, d), BF16)),
        grid=(lp // tm, f // tf),
        in_specs=[row, pl.BlockSpec((d, tf), lambda i, j: (0, j)), pl.BlockSpec((d, tf), lambda i, j: (0, j)),
                  pl.BlockSpec((tf, d), lambda i, j: (j, 0)), row, vec, vec],
        out_specs=(row, row),
        compiler_params=_cparams(("parallel", "arbitrary")),
        name="ffn0",
    )(xn, wg, wu, wd, h, gpost, gnext)


def _odd_proj_kernel(xn_ref, w_ref, b_ref, cos_ref, sin_ref, q_ref, k_ref, v_ref):
    z = _dot(xn_ref[...], w_ref[...]) + b_ref[...]
    cos, sin = cos_ref[...], sin_ref[...]
    sc = LOG2E / math.sqrt(C_HEAD_DIM)
    nq = q_ref.shape[1]
    nk = k_ref.shape[1]

    def rope(x):
        return x * cos + _half_swap(x, C_ROT // 2) * sin

    for c in range(0, nq, LANES):
        q_ref[:, c:c + LANES] = (rope(z[:, c:c + LANES]) * sc).astype(BF16)
    for c in range(0, nk, LANES):
        k_ref[:, c:c + LANES] = rope(z[:, nq + c:nq + c + LANES]).astype(BF16)
    v_ref[...] = z[:, nq + nk:].astype(BF16)


def _odd_proj(xn, w_p, b_p, cos, sin, tm):
    lp, d = xn.shape
    nq = C_HEADS * C_HEAD_DIM
    nk = C_KV_HEADS * LANES
    row = lambda n: pl.BlockSpec((tm, n), lambda i: (i, 0))
    full = lambda a: pl.BlockSpec(a.shape, lambda i: (0, 0))
    return pl.pallas_call(
        _odd_proj_kernel,
        out_shape=(jax.ShapeDtypeStruct((lp, nq), BF16), jax.ShapeDtypeStruct((lp, nk), BF16),
                   jax.ShapeDtypeStruct((lp, nk), BF16)),
        grid=(lp // tm,),
        in_specs=[row(d), full(w_p), full(b_p), row(LANES), row(LANES)],
        out_specs=(row(nq), row(nk), row(nk)),
        compiler_params=_cparams(("parallel",)),
        name="odd_proj",
    )(xn, w_p, b_p, cos, sin)


def _win_attn_kernel(sink_ref, q_ref, k0, k1, k2, k3, km, v0, v1, v2, v3, vm, o_ref, *, s_real, tq):
    g = pl.program_id(0)
    i = pl.program_id(1)
    is_meta = i == pl.num_programs(1) - 1
    nband = 4 * LANES
    kcat = jnp.concatenate([k0[...], k1[...], k2[...], k3[...], km[...]], axis=0)
    vcat = jnp.concatenate([v0[...], v1[...], v2[...], v3[...], vm[...]], axis=0)
    ncol = kcat.shape[0]
    a = lax.broadcasted_iota(jnp.int32, (tq, ncol), 0)
    c = lax.broadcasted_iota(jnp.int32, (tq, ncol), 1)
    kidx = jnp.where(is_meta, 0, i * tq - LANES) + c
    qidx = i * tq + a
    ninf = jnp.float32(-jnp.inf)
    b_real = jnp.where(jnp.abs(qidx - kidx) <= C_WINDOW, 0.0, ninf)
    b_real = jnp.where(kidx >= 0, b_real, ninf)
    b_real = jnp.where(kidx < s_real, b_real, ninf)
    b_meta_q = jnp.where(kidx <= (C_WINDOW - N_META) + a, 0.0, ninf)
    b_meta_k = jnp.where(c - nband < N_META, 0.0, ninf)
    bias = jnp.where(c < nband, jnp.where(is_meta, b_meta_q, b_real), b_meta_k)
    lane = lax.broadcasted_iota(jnp.int32, (tq, LANES), 1)
    heads_per_group = C_HEADS // C_KV_HEADS
    for pr in range(heads_per_group // 2):
        qp = q_ref[:, pr * LANES:(pr + 1) * LANES]
        halves = []
        for half in range(2):
            sink = sink_ref[g * heads_per_group + 2 * pr + half] * LOG2E
            qm = jnp.where((lane >= C_HEAD_DIM) == (half == 1), qp, jnp.zeros_like(qp))
            s = _dot_nt(qm, kcat) + bias
            m = jnp.maximum(jnp.max(s, axis=1, keepdims=True), sink)
            e = jnp.exp2(s - m)
            denom = jnp.sum(e, axis=1, keepdims=True) + jnp.exp2(sink - m)
            halves.append(_dot(e.astype(BF16), vcat) / denom)
        o_ref[:, pr * LANES:(pr + 1) * LANES] = jnp.where(lane < C_HEAD_DIM, halves[0], halves[1]).astype(o_ref.dtype)


def _win_attention(qc, kc, vc, sink, s_real, tq):
    lp = qc.shape[0]
    nq = lp // tq
    nkb = s_real // LANES
    per = tq // LANES
    gw = (C_HEADS // C_KV_HEADS) * C_HEAD_DIM

    def band(k):
        def index(g, i, sink_ref):
            b = jnp.where(i == nq - 1, k, jnp.clip(i * per - 1 + k, 0, nkb - 1))
            return (b, g)
        return pl.BlockSpec((LANES, LANES), index)

    meta = pl.BlockSpec((LANES, LANES), lambda g, i, sink_ref: (nkb, g))
    grid_spec = pltpu.PrefetchScalarGridSpec(
        num_scalar_prefetch=1,
        grid=(C_KV_HEADS, nq),
        in_specs=[pl.BlockSpec((tq, gw), lambda g, i, sink_ref: (i, g))]
                 + [band(k) for k in range(4)] + [meta] + [band(k) for k in range(4)] + [meta],
        out_specs=pl.BlockSpec((tq, gw), lambda g, i, sink_ref: (i, g)),
    )
    return pl.pallas_call(
        functools.partial(_win_attn_kernel, s_real=s_real, tq=tq),
        out_shape=jax.ShapeDtypeStruct((lp, C_HEADS * C_HEAD_DIM), BF16),
        grid_spec=grid_spec,
        compiler_params=_cparams(("parallel", "parallel")),
        name="win_attention",
    )(sink, qc, kc, kc, kc, kc, kc, vc, vc, vc, vc, vc)


def _out_proj1_kernel(o_ref, w_ref, b_ref, h_ref, gpost_ref, gnext_ref, router_ref, hn_ref, xn_ref, idx_ref, gate_ref):
    y = _dot(o_ref[...], w_ref[...]) + b_ref[...]
    h = h_ref[...] + _rms(y, gpost_ref[...])
    hn_ref[...] = h
    xn = _rms(h, gnext_ref[...])
    xn_ref[...] = xn
    logit = [jnp.sum(xn * router_ref[e:e + 1, :], axis=1, keepdims=True) for e in range(N_EXPERTS)]

    def top(cols):
        v = functools.reduce(jnp.maximum, cols)
        i = jnp.full_like(v, N_EXPERTS, dtype=jnp.int32)
        for e in reversed(range(N_EXPERTS)):
            i = jnp.where(cols[e] == v, e, i)
        return v, i

    v1, i1 = top(logit)
    v2, i2 = top([jnp.where(i1 == e, -jnp.inf, logit[e]) for e in range(N_EXPERTS)])
    e2 = jnp.exp(v2 - v1)
    g1 = 1.0 / (1.0 + e2)
    g2 = e2 / (1.0 + e2)
    lane = lax.broadcasted_iota(jnp.int32, idx_ref.shape, 1)
    idx_ref[...] = jnp.where(lane == 0, i1, jnp.where(lane == 1, i2, 0))
    gate_ref[...] = jnp.where(lane == 0, g1, jnp.where(lane == 1, g2, 0.0))


def _out_proj1(o, w, b, h, gpost, gnext, router_p, tm):
    lp, d = h.shape
    row = lambda n: pl.BlockSpec((tm, n), lambda i: (i, 0))
    full = lambda a: pl.BlockSpec(a.shape, lambda i: (0, 0))
    return pl.pallas_call(
        _out_proj1_kernel,
        out_shape=(jax.ShapeDtypeStruct((lp, d), F32), jax.ShapeDtypeStruct((lp, d), F32),
                   jax.ShapeDtypeStruct((lp, LANES), jnp.int32), jax.ShapeDtypeStruct((lp, LANES), F32)),
        grid=(lp // tm,),
        in_specs=[row(o.shape[1]), full(w), full(b), row(d), full(gpost), full(gnext), full(router_p)],
        out_specs=(row(d), row(d), row(LANES), row(LANES)),
        compiler_params=_cparams(("parallel",)),
        name="out_proj1",
    )(o, w, b, h, gpost, gnext, router_p)


def _gather_kernel(tok_ref, x_hbm, o_ref, buf, sem, *, tg):
    i = pl.program_id(0)
    n = pl.num_programs(0)

    def row_copy(step, slot, r):
        return pltpu.make_async_copy(x_hbm.at[pl.ds(tok_ref[step * tg + r], 1)], buf.at[slot, pl.ds(r, 1)],
                                     sem.at[slot])

    def start_all(step, slot):
        def body(r2, c):
            row_copy(step, slot, 2 * r2).start(priority=0)
            row_copy(step, slot, 2 * r2 + 1).start(priority=1)
            return c
        lax.fori_loop(0, tg // 2, body, 0, unroll=4)

    def wait_all(step, slot):
        def body(r, c):
            row_copy(step, slot, r).wait()
            return c
        lax.fori_loop(0, tg, body, 0, unroll=8)

    @pl.when(i == 0)
    def _():
        start_all(0, 0)

    for slot in range(2):
        @pl.when((i % 2 == slot) & (i + 1 < n))
        def _():
            start_all(i + 1, 1 - slot)

        @pl.when(i % 2 == slot)
        def _():
            wait_all(i, slot)
            o_ref[...] = buf[slot].astype(o_ref.dtype)


def _gather_rows(row_tok, x, tg):
    n_rows = row_tok.shape[0]
    d = x.shape[1]
    grid_spec = pltpu.PrefetchScalarGridSpec(
        num_scalar_prefetch=1,
        grid=(n_rows // tg,),
        in_specs=[pl.BlockSpec(memory_space=pl.ANY)],
        out_specs=pl.BlockSpec((tg, d), lambda i, tok: (i, 0)),
        scratch_shapes=[pltpu.VMEM((2, tg, d), x.dtype), pltpu.SemaphoreType.DMA((2,))],
    )
    return pl.pallas_call(
        functools.partial(_gather_kernel, tg=tg),
        out_shape=jax.ShapeDtypeStruct((n_rows, d), BF16),
        grid_spec=grid_spec,
        compiler_params=_cparams(("arbitrary",)),
        name="moe_gather",
    )(row_tok, x)


def _moe_ffn_kernel(te_ref, nu_ref, x_ref, wg_ref, wu_ref, wd_ref, y_ref):
    i = pl.program_id(0)
    j = pl.program_id(1)

    @pl.when(j == 0)
    def _():
        y_ref[...] = jnp.zeros_like(y_ref)

    @pl.when(i < nu_ref[0])
    def _():
        y_ref[...] += _swiglu_partial(x_ref[...], wg_ref.at[0], wu_ref.at[0], wd_ref.at[0])


def _moe_ffn(tile_expert, n_used, xs, wg, wu, wd, tm, tf):
    n_rows, d = xs.shape
    f = wg.shape[2]
    nf = f // tf

    def jj(i, j, nu):
        return jnp.where(i < nu[0], j, nf - 1)

    grid_spec = pltpu.PrefetchScalarGridSpec(
        num_scalar_prefetch=2,
        grid=(n_rows // tm, nf),
        in_specs=[pl.BlockSpec((tm, d), lambda i, j, te, nu: (i, 0)),
                  pl.BlockSpec((1, d, tf), lambda i, j, te, nu: (te[i], 0, jj(i, j, nu))),
                  pl.BlockSpec((1, d, tf), lambda i, j, te, nu: (te[i], 0, jj(i, j, nu))),
                  pl.BlockSpec((1, tf, d), lambda i, j, te, nu: (te[i], jj(i, j, nu), 0))],
        out_specs=pl.BlockSpec((tm, d), lambda i, j, te, nu: (i, 0)),
    )
    return pl.pallas_call(
        _moe_ffn_kernel,
        out_shape=jax.ShapeDtypeStruct((n_rows, d), F32),
        grid_spec=grid_spec,
        compiler_params=_cparams(("arbitrary", "arbitrary")),
        name="moe_ffn",
    )(tile_expert, n_used, xs, wg, wu, wd)


def _combine_kernel(dest_ref, y_hbm, h_ref, gate_ref, gpost_ref, o_ref, buf, sem, *, tc):
    i = pl.program_id(0)
    n = pl.num_programs(0)

    def row_copy(step, slot, r, k):
        return pltpu.make_async_copy(y_hbm.at[pl.ds(dest_ref[(step * tc + r) * TOP_K + k], 1)],
                                     buf.at[slot, k, pl.ds(r, 1)], sem.at[slot])

    def start_all(step, slot):
        def body(r, c):
            row_copy(step, slot, r, 0).start(priority=0)
            row_copy(step, slot, r, 1).start(priority=1)
            return c
        lax.fori_loop(0, tc, body, 0, unroll=4)

    def wait_all(step, slot):
        def body(r, c):
            row_copy(step, slot, r, 0).wait()
            row_copy(step, slot, r, 1).wait()
            return c
        lax.fori_loop(0, tc, body, 0, unroll=4)

    @pl.when(i == 0)
    def _():
        start_all(0, 0)

    for slot in range(2):
        @pl.when((i % 2 == slot) & (i + 1 < n))
        def _():
            start_all(i + 1, 1 - slot)

        @pl.when(i % 2 == slot)
        def _():
            wait_all(i, slot)
            gate = gate_ref[...]
            y = buf[slot, 0] * gate[:, 0:1] + buf[slot, 1] * gate[:, 1:2]
            o_ref[...] = h_ref[...] + _rms(y, gpost_ref[...])


def _combine(dest_flat, ys, h, gates, gpost, s_real, tc):
    d = h.shape[1]
    grid_spec = pltpu.PrefetchScalarGridSpec(
        num_scalar_prefetch=1,
        grid=(s_real // tc,),
        in_specs=[pl.BlockSpec(memory_space=pl.ANY),
                  pl.BlockSpec((tc, d), lambda i, dest: (i, 0)),
                  pl.BlockSpec((tc, LANES), lambda i, dest: (i, 0)),
                  pl.BlockSpec((1, d), lambda i, dest: (0, 0))],
        out_specs=pl.BlockSpec((tc, d), lambda i, dest: (i, 0)),
        scratch_shapes=[pltpu.VMEM((2, TOP_K, tc, d), ys.dtype), pltpu.SemaphoreType.DMA((2,))],
    )
    return pl.pallas_call(
        functools.partial(_combine_kernel, tc=tc),
        out_shape=jax.ShapeDtypeStruct((s_real, d), F32),
        grid_spec=grid_spec,
        compiler_params=_cparams(("arbitrary",)),
        name="moe_combine",
    )(dest_flat, ys, h, gates, gpost)


def _rope_angles(pos, dim, theta):
    inv_freq = theta ** (-jnp.arange(0, dim, 2, dtype=F32) / dim)
    return pos.astype(F32)[:, None] * inv_freq[None, :]


def _tables(s_real, lp):
    r = jnp.arange(lp, dtype=jnp.int32)
    is_real = r < s_real
    p = r - s_real
    pos = jnp.where(is_real, r + N_META, jnp.clip(p, 0, N_META - 1))
    grid_row = jnp.where(is_real, r // GRID_W, -1)
    grid_col = jnp.where(is_real, r % GRID_W, jnp.clip(p, 0, N_META - 1))

    def cs(ang):
        c, s = jnp.cos(ang), jnp.sin(ang)
        return jnp.concatenate([c, c], axis=1), jnp.concatenate([-s, s], axis=1)

    a_c, a_s = cs(_rope_angles(pos, A_ROPE, ROPE_THETA))
    pad_c = jnp.ones((lp, LANES - A_ROPE), F32)
    acos = jnp.concatenate([a_c, pad_c], axis=1)
    asin = jnp.concatenate([a_s, 0.0 * pad_c], axis=1)
    r_c, r_s = cs(_rope_angles(grid_row, B_HEAD_DIM // 2, AXIAL_THETA))
    c_c, c_s = cs(_rope_angles(grid_col, B_HEAD_DIM // 2, AXIAL_THETA))
    bcos = jnp.concatenate([r_c, c_c], axis=1)
    bsin = jnp.concatenate([r_s, c_s], axis=1)
    w_c, w_s = cs(_rope_angles(pos, C_ROT, ROPE_THETA))
    one = jnp.ones((lp, C_HEAD_DIM - C_ROT), F32)
    ccos = jnp.concatenate([w_c, one, w_c, one], axis=1)
    csin = jnp.concatenate([w_s, 0.0 * one, w_s, 0.0 * one], axis=1)
    return acos, asin, bcos, bsin, ccos, csin


def _route(idx, tm, n_tiles):
    e_flat = idx.reshape(-1)
    n_assign = e_flat.shape[0]
    experts = jnp.arange(N_EXPERTS, dtype=jnp.int32)
    onehot = (e_flat[:, None] == experts[None, :]).astype(jnp.int32)
    csum = jnp.cumsum(onehot, axis=0)
    counts = csum[-1]
    padded = (counts + tm - 1) // tm * tm
    start = jnp.cumsum(counts) - counts
    pend = jnp.cumsum(padded)
    pstart = pend - padded
    dest_flat = jnp.sum(onehot * (pstart[None, :] + csum - 1), axis=1)
    order = jnp.argsort(e_flat * n_assign + jnp.arange(n_assign, dtype=jnp.int32))
    rows = jnp.arange(n_tiles * tm, dtype=jnp.int32)
    row_expert = jnp.minimum(jnp.searchsorted(pend, rows, side='right'), N_EXPERTS - 1)
    rank = rows - pstart[row_expert]
    assign = order[jnp.clip(start[row_expert] + rank, 0, n_assign - 1)]
    row_tok = jnp.where(rank < counts[row_expert], assign // TOP_K, rows % (n_assign // TOP_K)).astype(jnp.int32)
    tile_expert = row_expert[::tm].astype(jnp.int32)
    n_used = (pend[-1] // tm).astype(jnp.int32).reshape(1)
    return row_tok, dest_flat, tile_expert, n_used


def kernel(x, meta_tokens, even_attn_pre, even_attn_post, even_w_in, a_q_norm, a_wq_b, a_kv_norm, a_wkv_b, b_q_norm, b_k_norm, even_w_out, even_ffn_pre, even_ffn_post, ffn_w_gate, ffn_w_up, ffn_w_down, odd_attn_pre, odd_attn_post, c_w_qkv, c_b_qkv, c_sink, c_w_out, c_b_out, odd_ffn_pre, odd_ffn_post, moe_router, moe_w_gate, moe_w_up, moe_w_down):
    bsz, s_real, d = x.shape
    assert bsz == 1 and s_real % ROW_PAD == 0 and s_real % GRID_W == 0
    lp = s_real + ROW_PAD
    n_tok = s_real + N_META
    vec = lambda g: g.reshape(1, -1).astype(F32)
    acos, asin, bcos, bsin, ccos, csin = _tables(s_real, lp)
    tm_proj = _row_tile(lp, 640)

    h, xn = _prep(x[0], meta_tokens, vec(even_attn_pre[0]), lp)

    w_in = even_w_in[0]
    w_in_p = jnp.concatenate([w_in[:, :A_IN], jnp.zeros((d, LANES - A_ROPE), F32), w_in[:, A_IN:]], axis=1).astype(BF16)
    wq_p = jnp.pad(a_wq_b[0].reshape(A_Q_RANK, A_HEADS, A_NOPE + A_ROPE),
                   ((0, 0), (0, 0), (0, A_QK - A_NOPE - A_ROPE))).reshape(A_Q_RANK, A_HEADS * A_QK).astype(BF16)
    qaT, kn, vaT, krot, qbT, kb, vbT = _even_proj(
        xn, w_in_p, vec(a_q_norm[0]), wq_p, vec(a_kv_norm[0]), a_wkv_b[0].astype(BF16),
        vec(b_q_norm[0]), vec(b_k_norm[0]), acos, asin, bcos, bsin, _row_tile(lp, 256))

    tk = _key_chunk(s_real, 1024)
    oa = _mla_attention(qaT, kn, krot, vaT, s_real, _row_tile(lp, 1280), tk)
    ob = _gqa_attention(qbT, kb, vbT, s_real, _row_tile(lp, 512), tk)

    w_out = even_w_out[0].astype(BF16)
    n_a = A_HEADS * A_V
    h, xn = _out_proj0(oa, ob, w_out[:n_a], w_out[n_a:], h, vec(even_attn_post[0]), vec(even_ffn_pre[0]), tm_proj)

    tf = 512 if ffn_w_gate.shape[2] % 512 == 0 else LANES
    h, xn = _ffn0(xn, ffn_w_gate[0].astype(BF16), ffn_w_up[0].astype(BF16), ffn_w_down[0].astype(BF16),
                  h, vec(even_ffn_post[0]), vec(odd_attn_pre[0]), _row_tile(lp, 640), tf)

    nq = C_HEADS * C_HEAD_DIM
    nkv = C_KV_HEADS * C_HEAD_DIM

    def dup_heads(w):
        w4 = w.reshape(w.shape[:-1] + (C_KV_HEADS, C_HEAD_DIM))
        return jnp.concatenate([w4, w4], axis=-1).reshape(w.shape[:-1] + (C_KV_HEADS * LANES,))

    wc, bc = c_w_qkv[0], c_b_qkv[0]
    wc_p = jnp.concatenate([wc[:, :nq], dup_heads(wc[:, nq:nq + nkv]), dup_heads(wc[:, nq + nkv:])], axis=1).astype(BF16)
    bc_p = jnp.concatenate([bc[:nq], dup_heads(bc[nq:nq + nkv]), dup_heads(bc[nq + nkv:])]).reshape(1, -1)
    qc, kc, vc = _odd_proj(xn, wc_p, bc_p, ccos, csin, tm_proj)
    oc = _win_attention(qc, kc, vc, c_sink[0].astype(F32), s_real, ROW_PAD)

    h, xn_f32, idx_l, gate_l = _out_proj1(oc, c_w_out[0].astype(BF16), vec(c_b_out[0]), h, vec(odd_attn_post[0]),
                                          vec(odd_ffn_pre[0]), moe_router[0].T, tm_proj)

    tm_moe = 1024 if s_real >= 8192 else 256
    n_assign = n_tok * TOP_K
    n_tiles = -(-(n_assign + N_EXPERTS * (tm_moe - 1)) // tm_moe)
    row_tok, dest_flat, tile_expert, n_used = _route(idx_l[:n_tok, :TOP_K], tm_moe, n_tiles)
    xs = _gather_rows(row_tok, xn_f32, 1024 if tm_moe % 1024 == 0 else 256)
    tf_moe = 512 if moe_w_gate.shape[3] % 512 == 0 else LANES
    ys = _moe_ffn(tile_expert, n_used, xs, moe_w_gate[0], moe_w_up[0], moe_w_down[0], tm_moe, tf_moe)
    out = _combine(dest_flat, ys, h, gate_l, vec(odd_ffn_post[0]), s_real, 256)
    return out[None]
```

```python
import functools
import math

import jax
import jax.numpy as jnp
from jax import lax
from jax.experimental import pallas as pl
from jax.experimental.pallas import tpu as pltpu

F32 = jnp.float32
BF16 = jnp.bfloat16

N_META = 16
GRID_W = 64
RMS_EPS = 1e-6
ROPE_THETA = 500000.0
AXIAL_THETA = 10000.0
A_HEADS, A_Q_RANK, A_KV_RANK, A_NOPE, A_ROPE, A_V = 8, 512, 256, 128, 64, 128
A_IN = A_Q_RANK + A_KV_RANK + A_ROPE
B_HEADS, B_KV_HEADS, B_HEAD_DIM = 8, 2, 128
C_HEADS, C_KV_HEADS, C_HEAD_DIM, C_WINDOW = 32, 4, 64, 128
C_ROT = C_HEAD_DIM // 4
N_EXPERTS, TOP_K = 8, 2

LANES = 128
V7X_VMEM_LIMIT = 56 * 1024 * 1024
LOG2E = math.log2(math.e)

ROW_PAD = 256
ONES_ROWS = 16
A_QK = 2 * LANES


def _cparams(sem):
    return pltpu.CompilerParams(dimension_semantics=sem, vmem_limit_bytes=V7X_VMEM_LIMIT)


def _row_tile(n_rows, target):
    best = LANES
    t = LANES
    while t <= min(n_rows, target):
        if n_rows % t == 0:
            best = t
        t += LANES
    return best


def _rms(x, g):
    ms = jnp.mean(x * x, axis=-1, keepdims=True)
    return x * lax.rsqrt(ms + RMS_EPS) * g


def _dot(a, b):
    return jnp.dot(a, b, preferred_element_type=F32)


def _dot_nt(a, b):
    return lax.dot_general(a, b, (((1,), (1,)), ((), ())), preferred_element_type=F32)


def _half_swap(x, half):
    lane = lax.broadcasted_iota(jnp.int32, x.shape, x.ndim - 1)
    up = pltpu.roll(x, x.shape[-1] - half, x.ndim - 1)
    down = pltpu.roll(x, half, x.ndim - 1)
    return jnp.where((lane & half) == 0, up, down)


def _prep_kernel(x_ref, meta_ref, g_ref, h_ref, xn_ref, *, n_real_blocks):
    i = pl.program_id(0)

    @pl.when(i < n_real_blocks)
    def _():
        h = x_ref[...]
        h_ref[...] = h
        xn_ref[...] = _rms(h, g_ref[...]).astype(xn_ref.dtype)

    @pl.when(i >= n_real_blocks)
    def _():
        h_ref[...] = jnp.zeros_like(h_ref)
        xn_ref[...] = jnp.zeros_like(xn_ref)
        m = meta_ref[...]
        h_ref[0:N_META, :] = m
        xn_ref[0:N_META, :] = _rms(m, g_ref[...]).astype(xn_ref.dtype)


def _prep(x2d, meta, g, lp):
    s, d = x2d.shape
    tm = ROW_PAD
    nrb = s // tm
    return pl.pallas_call(
        functools.partial(_prep_kernel, n_real_blocks=nrb),
        out_shape=(jax.ShapeDtypeStruct((lp, d), F32), jax.ShapeDtypeStruct((lp, d), BF16)),
        grid=(lp // tm,),
        in_specs=[pl.BlockSpec((tm, d), lambda i: (jnp.minimum(i, nrb - 1), 0)),
                  pl.BlockSpec((N_META, d), lambda i: (0, 0)),
                  pl.BlockSpec((1, d), lambda i: (0, 0))],
        out_specs=(pl.BlockSpec((tm, d), lambda i: (i, 0)), pl.BlockSpec((tm, d), lambda i: (i, 0))),
        compiler_params=_cparams(("parallel",)),
        name="prep",
    )(x2d, meta, g)


def _even_proj_kernel(xn_ref, w_in_ref, gq_ref, wq_ref, gkv_ref, wkv_ref, bgq_ref, bgk_ref,
                      acos_ref, asin_ref, bcos_ref, bsin_ref,
                      qaT_ref, kn_ref, vaT_ref, krot_ref, qbT_ref, kb_ref, vbT_ref):
    z = _dot(xn_ref[...], w_in_ref[...])
    acos, asin = acos_ref[...], asin_ref[...]
    bcos, bsin = bcos_ref[...], bsin_ref[...]
    sa = LOG2E / math.sqrt(A_NOPE + A_ROPE)
    sb = LOG2E / math.sqrt(B_HEAD_DIM)
    o = 0
    cqn = _rms(z[:, o:o + A_Q_RANK], gq_ref[...]).astype(BF16)
    o += A_Q_RANK
    qa = _dot(cqn, wq_ref[...])
    for h in range(A_HEADS):
        c = h * A_QK
        qaT_ref[c:c + LANES, :] = (qa[:, c:c + LANES] * sa).T.astype(BF16)
        rot = qa[:, c + LANES:c + A_QK]
        rot = rot * acos + _half_swap(rot, A_ROPE // 2) * asin
        qaT_ref[c + LANES:c + A_QK, :] = (rot * sa).T.astype(BF16)
    ckvn = _rms(z[:, o:o + A_KV_RANK], gkv_ref[...]).astype(BF16)
    o += A_KV_RANK
    kv = _dot(ckvn, wkv_ref[...])
    for h in range(A_HEADS):
        c = h * (A_NOPE + A_V)
        kn_ref[:, h * A_NOPE:(h + 1) * A_NOPE] = kv[:, c:c + A_NOPE].astype(BF16)
        vaT_ref[h * A_V:(h + 1) * A_V, :] = kv[:, c + A_NOPE:c + A_NOPE + A_V].T.astype(BF16)
    kr = z[:, o:o + LANES]
    o += LANES
    krot_ref[...] = (kr * acos + _half_swap(kr, A_ROPE // 2) * asin).astype(BF16)

    def norm_rope(x, g):
        xn = _rms(x, g)
        return xn * bcos + _half_swap(xn, B_HEAD_DIM // 4) * bsin

    for h in range(B_HEADS):
        c = h * B_HEAD_DIM
        qbT_ref[c:c + B_HEAD_DIM, :] = (norm_rope(z[:, o + c:o + c + B_HEAD_DIM], bgq_ref[...]) * sb).T.astype(BF16)
    o += B_HEADS * B_HEAD_DIM
    for h in range(B_KV_HEADS):
        c = h * B_HEAD_DIM
        kb_ref[:, c:c + B_HEAD_DIM] = norm_rope(z[:, o + c:o + c + B_HEAD_DIM], bgk_ref[...]).astype(BF16)
    o += B_KV_HEADS * B_HEAD_DIM
    for h in range(B_KV_HEADS):
        c = h * B_HEAD_DIM
        vbT_ref[c:c + B_HEAD_DIM, :] = z[:, o + c:o + c + B_HEAD_DIM].T.astype(BF16)


def _even_proj(xn, w_in_p, gq, wq_p, gkv, wkv, bgq, bgk, acos, asin, bcos, bsin, tm):
    lp, d = xn.shape
    row = lambda n: pl.BlockSpec((tm, n), lambda i: (i, 0))
    col = lambda n: pl.BlockSpec((n, tm), lambda i: (0, i))
    full = lambda a: pl.BlockSpec(a.shape, lambda i: (0, 0))
    outs = (("t", A_HEADS * A_QK), ("n", A_HEADS * A_NOPE), ("t", A_HEADS * A_V), ("n", LANES),
            ("t", B_HEADS * B_HEAD_DIM), ("n", B_KV_HEADS * B_HEAD_DIM), ("t", B_KV_HEADS * B_HEAD_DIM))
    return pl.pallas_call(
        _even_proj_kernel,
        out_shape=tuple(jax.ShapeDtypeStruct((n, lp) if k == "t" else (lp, n), BF16) for k, n in outs),
        grid=(lp // tm,),
        in_specs=[row(d), full(w_in_p), full(gq), full(wq_p), full(gkv), full(wkv), full(bgq), full(bgk),
                  row(LANES), row(LANES), row(LANES), row(LANES)],
        out_specs=tuple(col(n) if k == "t" else row(n) for k, n in outs),
        compiler_params=_cparams(("parallel",)),
        name="even_proj",
    )(xn, w_in_p, gq, wq_p, gkv, wkv, bgq, bgk, acos, asin, bcos, bsin)


def _flash_t(qT, k_chunk, vT_chunk, k_meta, vT_meta, m_sc, acc_sc, s_sc, p_sc, mx_sc, al_sc, *, n_chunks):
    def with_ones(vT):
        return jnp.concatenate([vT, jnp.ones((ONES_ROWS, vT.shape[1]), vT.dtype)], axis=0)

    km, vTm = k_meta(), vT_meta()
    s = _dot(km, qT)
    key = lax.broadcasted_iota(jnp.int32, s.shape, 0)
    s = jnp.where(key < N_META, s, -jnp.inf)
    m = jnp.max(s, axis=0, keepdims=True)
    m_sc[...] = m
    acc_sc[...] = _dot(with_ones(vTm), jnp.exp2(s - m).astype(BF16))

    nq = qT.shape[1]

    def scores(c, b):
        s = _dot(k_chunk(c), qT)
        s_sc[b, :, :nq] = s
        mx_sc[b] = jnp.max(s, axis=0, keepdims=True)

    def probs(b):
        m_prev = m_sc[...]
        m_new = jnp.maximum(m_prev, mx_sc[b])
        al_sc[b] = jnp.exp2(m_prev - m_new)
        m_sc[...] = m_new
        p_sc[b, :, :nq] = jnp.exp2(s_sc[b, :, :nq] - m_new).astype(BF16)

    def accumulate(c, b):
        acc_sc[...] = al_sc[b] * acc_sc[...] + _dot(with_ones(vT_chunk(c)), p_sc[b, :, :nq])

    assert n_chunks % 2 == 0
    scores(0, 0)
    scores(1, 1)
    probs(0)

    def step(i, carry):
        c = 2 * i
        scores(c + 2, 0)
        probs(1)
        accumulate(c, 0)
        scores(c + 3, 1)
        probs(0)
        accumulate(c + 1, 1)
        return carry

    lax.fori_loop(0, n_chunks // 2 - 1, step, 0)
    probs(1)
    accumulate(n_chunks - 2, 0)
    accumulate(n_chunks - 1, 1)


def _flash_scratch(dv, nq, tk):
    wide = nq if (nq // LANES) % 2 else nq + LANES
    return [pltpu.VMEM((1, nq), F32), pltpu.VMEM((dv + ONES_ROWS, nq), F32),
            pltpu.VMEM((2, tk, wide), F32), pltpu.VMEM((2, tk, wide), BF16),
            pltpu.VMEM((2, 1, nq), F32), pltpu.VMEM((2, 1, nq), F32)]


def _key_chunk(s_real, target):
    best = None
    for t in range(LANES, min(s_real // 2, target) + 1, LANES):
        if s_real % (2 * t) == 0:
            best = t
    assert best is not None
    return best


def _mla_attn_kernel(qT_ref, kn_ref, kr_ref, vT_ref, knm_ref, krm_ref, vTm_ref, o_ref, *scratch, tk):
    rows = lambda c: pl.ds(pl.multiple_of(c * tk, tk), tk)
    _flash_t(qT_ref[...],
             lambda c: jnp.concatenate([kn_ref[rows(c), :], kr_ref[rows(c), :]], axis=1),
             lambda c: vT_ref[:, rows(c)],
             lambda: jnp.concatenate([knm_ref[...], krm_ref[...]], axis=1),
             lambda: vTm_ref[...],
             *scratch, n_chunks=kn_ref.shape[0] // tk)
    acc_sc = scratch[1]
    o_ref[...] = (acc_sc[:A_V, :] / acc_sc[A_V:A_V + 1, :]).T.astype(o_ref.dtype)


def _mla_attention(qaT, kn, krot, vaT, s_real, tq, tk):
    lp = qaT.shape[1]
    mb = s_real // LANES
    return pl.pallas_call(
        functools.partial(_mla_attn_kernel, tk=tk),
        out_shape=jax.ShapeDtypeStruct((lp, A_HEADS * A_V), BF16),
        grid=(A_HEADS, lp // tq),
        in_specs=[pl.BlockSpec((A_QK, tq), lambda h, i: (h, i)),
                  pl.BlockSpec((s_real, A_NOPE), lambda h, i: (0, h)),
                  pl.BlockSpec((s_real, LANES), lambda h, i: (0, 0)),
                  pl.BlockSpec((A_V, s_real), lambda h, i: (h, 0)),
                  pl.BlockSpec((LANES, A_NOPE), lambda h, i: (mb, h)),
                  pl.BlockSpec((LANES, LANES), lambda h, i: (mb, 0)),
                  pl.BlockSpec((A_V, LANES), lambda h, i: (h, mb))],
        out_specs=pl.BlockSpec((tq, A_V), lambda h, i: (i, h)),
        scratch_shapes=_flash_scratch(A_V, tq, tk),
        compiler_params=_cparams(("parallel", "parallel")),
        name="mla_attention",
    )(qaT, kn, krot, vaT, kn, krot, vaT)


def _gqa_attn_kernel(qT_ref, k_ref, vT_ref, km_ref, vTm_ref, o_ref, *scratch, tk, n_stack):
    dh = k_ref.shape[1]
    tq = qT_ref.shape[1]
    rows = lambda c: pl.ds(pl.multiple_of(c * tk, tk), tk)
    qT = jnp.concatenate([qT_ref[j * dh:(j + 1) * dh, :] for j in range(n_stack)], axis=1)
    _flash_t(qT, lambda c: k_ref[rows(c), :], lambda c: vT_ref[:, rows(c)], lambda: km_ref[...], lambda: vTm_ref[...],
             *scratch, n_chunks=k_ref.shape[0] // tk)
    acc_sc = scratch[1]
    for j in range(n_stack):
        sl = slice(j * tq, (j + 1) * tq)
        o_ref[:, j * dh:(j + 1) * dh] = (acc_sc[:dh, sl] / acc_sc[dh:dh + 1, sl]).T.astype(o_ref.dtype)


def _gqa_attention(qbT, kb, vbT, s_real, tq, tk):
    lp = qbT.shape[1]
    grp = B_HEADS // B_KV_HEADS
    dh = B_HEAD_DIM
    mb = s_real // LANES
    return pl.pallas_call(
        functools.partial(_gqa_attn_kernel, tk=tk, n_stack=grp),
        out_shape=jax.ShapeDtypeStruct((lp, B_HEADS * dh), BF16),
        grid=(B_KV_HEADS, lp // tq),
        in_specs=[pl.BlockSpec((grp * dh, tq), lambda g, i: (g, i)),
                  pl.BlockSpec((s_real, dh), lambda g, i: (0, g)),
                  pl.BlockSpec((dh, s_real), lambda g, i: (g, 0)),
                  pl.BlockSpec((LANES, dh), lambda g, i: (mb, g)),
                  pl.BlockSpec((dh, LANES), lambda g, i: (g, mb))],
        out_specs=pl.BlockSpec((tq, grp * dh), lambda g, i: (i, g)),
        scratch_shapes=_flash_scratch(dh, grp * tq, tk),
        compiler_params=_cparams(("parallel", "parallel")),
        name="gqa_attention",
    )(qbT, kb, vbT, kb, vbT)


def _out_proj0_kernel(oa_ref, ob_ref, wa_ref, wb_ref, h_ref, gpost_ref, gnext_ref, hn_ref, xn_ref):
    y = _dot(oa_ref[...], wa_ref[...]) + _dot(ob_ref[...], wb_ref[...])
    h = h_ref[...] + _rms(y, gpost_ref[...])
    hn_ref[...] = h
    xn_ref[...] = _rms(h, gnext_ref[...]).astype(xn_ref.dtype)


def _out_proj0(oa, ob, wa, wb, h, gpost, gnext, tm):
    lp, d = h.shape
    row = lambda n: pl.BlockSpec((tm, n), lambda i: (i, 0))
    full = lambda a: pl.BlockSpec(a.shape, lambda i: (0, 0))
    return pl.pallas_call(
        _out_proj0_kernel,
        out_shape=(jax.ShapeDtypeStruct((lp, d), F32), jax.ShapeDtypeStruct((lp, d), BF16)),
        grid=(lp // tm,),
        in_specs=[row(oa.shape[1]), row(ob.shape[1]), full(wa), full(wb), row(d), full(gpost), full(gnext)],
        out_specs=(row(d), row(d)),
        compiler_params=_cparams(("parallel",)),
        name="out_proj0",
    )(oa, ob, wa, wb, h, gpost, gnext)


def _swiglu_accumulate(y_ref, x_ref, wg_ref, wu_ref, wd_ref):
    x = x_ref[...]
    g = _dot(x, wg_ref[...].astype(BF16))
    u = _dot(x, wu_ref[...].astype(BF16))
    a = (g * jax.nn.sigmoid(g) * u).astype(BF16)
    y_ref[...] += _dot(a, wd_ref[...].astype(BF16))


def _ffn0_kernel(x_ref, wg_ref, wu_ref, wd_ref, h_ref, gpost_ref, gnext_ref, hn_ref, xn_ref):
    j = pl.program_id(1)

    @pl.when(j == 0)
    def _():
        hn_ref[...] = jnp.zeros_like(hn_ref)

    _swiglu_accumulate(hn_ref, x_ref, wg_ref, wu_ref, wd_ref)

    @pl.when(j == pl.num_programs(1) - 1)
    def _():
        h = h_ref[...] + _rms(hn_ref[...], gpost_ref[...])
        hn_ref[...] = h
        xn_ref[...] = _rms(h, gnext_ref[...]).astype(xn_ref.dtype)


def _ffn0(xn, wg, wu, wd, h, gpost, gnext, tm, tf):
    lp, d = h.shape
    f = wg.shape[1]
    row = pl.BlockSpec((tm, d), lambda i, j: (i, 0))
    vec = pl.BlockSpec((1, d), lambda i, j: (0, 0))
    return pl.pallas_call(
        _ffn0_kernel,
        out_shape=(jax.ShapeDtypeStruct((lp, d), F32), jax.ShapeDtypeStruct((lp, d), BF16)),
        grid=(lp // tm, f // tf),
        in_specs=[row, pl.BlockSpec((d, tf), lambda i, j: (0, j)), pl.BlockSpec((d, tf), lambda i, j: (0, j)),
                  pl.BlockSpec((tf, d), lambda i, j: (j, 0)), row, vec, vec],
        out_specs=(row, row),
        compiler_params=_cparams(("parallel", "arbitrary")),
        name="ffn0",
    )(xn, wg, wu, wd, h, gpost, gnext)


def _odd_proj_kernel(xn_ref, w_ref, b_ref, cos_ref, sin_ref, q_ref, k_ref, v_ref):
    z = _dot(xn_ref[...], w_ref[...]) + b_ref[...]
    cos, sin = cos_ref[...], sin_ref[...]
    sc = LOG2E / math.sqrt(C_HEAD_DIM)
    nq = q_ref.shape[1]
    nk = k_ref.shape[1]

    def rope(x):
        return x * cos + _half_swap(x, C_ROT // 2) * sin

    for c in range(0, nq, LANES):
        q_ref[:, c:c + LANES] = (rope(z[:, c:c + LANES]) * sc).astype(BF16)
    for c in range(0, nk, LANES):
        k_ref[:, c:c + LANES] = rope(z[:, nq + c:nq + c + LANES]).astype(BF16)
    v_ref[...] = z[:, nq + nk:].astype(BF16)


def _odd_proj(xn, w_p, b_p, cos, sin, tm):
    lp, d = xn.shape
    nq = C_HEADS * C_HEAD_DIM
    nk = C_KV_HEADS * LANES
    row = lambda n: pl.BlockSpec((tm, n), lambda i: (i, 0))
    full = lambda a: pl.BlockSpec(a.shape, lambda i: (0, 0))
    return pl.pallas_call(
        _odd_proj_kernel,
        out_shape=(jax.ShapeDtypeStruct((lp, nq), BF16), jax.ShapeDtypeStruct((lp, nk), BF16),
                   jax.ShapeDtypeStruct((lp, nk), BF16)),
        grid=(lp // tm,),
        in_specs=[row(d), full(w_p), full(b_p), row(LANES), row(LANES)],
        out_specs=(row(nq), row(nk), row(nk)),
        compiler_params=_cparams(("parallel",)),
        name="odd_proj",
    )(xn, w_p, b_p, cos, sin)


def _win_attn_kernel(sink_ref, q_ref, k0, k1, k2, k3, km, v0, v1, v2, v3, vm, o_ref, *, s_real, tq):
    g = pl.program_id(0)
    i = pl.program_id(1)
    is_meta = i == pl.num_programs(1) - 1
    nband = 4 * LANES
    kcat = jnp.concatenate([k0[...], k1[...], k2[...], k3[...], km[...]], axis=0)
    vcat = jnp.concatenate([v0[...], v1[...], v2[...], v3[...], vm[...]], axis=0)
    ncol = kcat.shape[0]
    a = lax.broadcasted_iota(jnp.int32, (tq, ncol), 0)
    c = lax.broadcasted_iota(jnp.int32, (tq, ncol), 1)
    kidx = jnp.where(is_meta, 0, i * tq - LANES) + c
    qidx = i * tq + a
    ninf = jnp.float32(-jnp.inf)
    b_real = jnp.where(jnp.abs(qidx - kidx) <= C_WINDOW, 0.0, ninf)
    b_real = jnp.where(kidx >= 0, b_real, ninf)
    b_real = jnp.where(kidx < s_real, b_real, ninf)
    b_meta_q = jnp.where(kidx <= (C_WINDOW - N_META) + a, 0.0, ninf)
    b_meta_k = jnp.where(c - nband < N_META, 0.0, ninf)
    bias = jnp.where(c < nband, jnp.where(is_meta, b_meta_q, b_real), b_meta_k)
    lane = lax.broadcasted_iota(jnp.int32, (tq, LANES), 1)
    heads_per_group = C_HEADS // C_KV_HEADS
    for pr in range(heads_per_group // 2):
        qp = q_ref[:, pr * LANES:(pr + 1) * LANES]
        halves = []
        for half in range(2):
            sink = sink_ref[g * heads_per_group + 2 * pr + half] * LOG2E
            qm = jnp.where((lane >= C_HEAD_DIM) == (half == 1), qp, jnp.zeros_like(qp))
            s = _dot_nt(qm, kcat) + bias
            m = jnp.maximum(jnp.max(s, axis=1, keepdims=True), sink)
            e = jnp.exp2(s - m)
            denom = jnp.sum(e, axis=1, keepdims=True) + jnp.exp2(sink - m)
            halves.append(_dot(e.astype(BF16), vcat) / denom)
        o_ref[:, pr * LANES:(pr + 1) * LANES] = jnp.where(lane < C_HEAD_DIM, halves[0], halves[1]).astype(o_ref.dtype)


def _win_attention(qc, kc, vc, sink, s_real, tq):
    lp = qc.shape[0]
    nq = lp // tq
    nkb = s_real // LANES
    per = tq // LANES
    gw = (C_HEADS // C_KV_HEADS) * C_HEAD_DIM

    def band(k):
        def index(g, i, sink_ref):
            b = jnp.where(i == nq - 1, k, jnp.clip(i * per - 1 + k, 0, nkb - 1))
            return (b, g)
        return pl.BlockSpec((LANES, LANES), index)

    meta = pl.BlockSpec((LANES, LANES), lambda g, i, sink_ref: (nkb, g))
    grid_spec = pltpu.PrefetchScalarGridSpec(
        num_scalar_prefetch=1,
        grid=(C_KV_HEADS, nq),
        in_specs=[pl.BlockSpec((tq, gw), lambda g, i, sink_ref: (i, g))]
                 + [band(k) for k in range(4)] + [meta] + [band(k) for k in range(4)] + [meta],
        out_specs=pl.BlockSpec((tq, gw), lambda g, i, sink_ref: (i, g)),
    )
    return pl.pallas_call(
        functools.partial(_win_attn_kernel, s_real=s_real, tq=tq),
        out_shape=jax.ShapeDtypeStruct((lp, C_HEADS * C_HEAD_DIM), BF16),
        grid_spec=grid_spec,
        compiler_params=_cparams(("parallel", "parallel")),
        name="win_attention",
    )(sink, qc, kc, kc, kc, kc, kc, vc, vc, vc, vc, vc)


def _out_proj1_kernel(o_ref, w_ref, b_ref, h_ref, gpost_ref, gnext_ref, router_ref, hn_ref, xn_ref, idx_ref, gate_ref):
    y = _dot(o_ref[...], w_ref[...]) + b_ref[...]
    h = h_ref[...] + _rms(y, gpost_ref[...])
    hn_ref[...] = h
    xn = _rms(h, gnext_ref[...])
    xn_ref[...] = xn
    logit = [jnp.sum(xn * router_ref[e:e + 1, :], axis=1, keepdims=True) for e in range(N_EXPERTS)]

    def top(cols):
        v = functools.reduce(jnp.maximum, cols)
        i = jnp.full_like(v, N_EXPERTS, dtype=jnp.int32)
        for e in reversed(range(N_EXPERTS)):
            i = jnp.where(cols[e] == v, e, i)
        return v, i

    v1, i1 = top(logit)
    v2, i2 = top([jnp.where(i1 == e, -jnp.inf, logit[e]) for e in range(N_EXPERTS)])
    e2 = jnp.exp(v2 - v1)
    g1 = 1.0 / (1.0 + e2)
    g2 = e2 / (1.0 + e2)
    lane = lax.broadcasted_iota(jnp.int32, idx_ref.shape, 1)
    idx_ref[...] = jnp.where(lane == 0, i1, jnp.where(lane == 1, i2, 0))
    gate_ref[...] = jnp.where(lane == 0, g1, jnp.where(lane == 1, g2, 0.0))


def _out_proj1(o, w, b, h, gpost, gnext, router_p, tm):
    lp, d = h.shape
    row = lambda n: pl.BlockSpec((tm, n), lambda i: (i, 0))
    full = lambda a: pl.BlockSpec(a.shape, lambda i: (0, 0))
    return pl.pallas_call(
        _out_proj1_kernel,
        out_shape=(jax.ShapeDtypeStruct((lp, d), F32), jax.ShapeDtypeStruct((lp, d), F32),
                   jax.ShapeDtypeStruct((lp, LANES), jnp.int32), jax.ShapeDtypeStruct((lp, LANES), F32)),
        grid=(lp // tm,),
        in_specs=[row(o.shape[1]), full(w), full(b), row(d), full(gpost), full(gnext), full(router_p)],
        out_specs=(row(d), row(d), row(LANES), row(LANES)),
        compiler_params=_cparams(("parallel",)),
        name="out_proj1",
    )(o, w, b, h, gpost, gnext, router_p)


def _gather_kernel(tok_ref, x_hbm, o_ref, buf, sem, *, tg):
    i = pl.program_id(0)
    n = pl.num_programs(0)

    def row_copy(step, slot, r):
        return pltpu.make_async_copy(x_hbm.at[pl.ds(tok_ref[step * tg + r], 1)], buf.at[slot, pl.ds(r, 1)],
                                     sem.at[slot])

    def start_all(step, slot):
        def body(r2, c):
            row_copy(step, slot, 2 * r2).start(priority=0)
            row_copy(step, slot, 2 * r2 + 1).start(priority=1)
            return c
        lax.fori_loop(0, tg // 2, body, 0, unroll=4)

    def wait_all(step, slot):
        def body(r, c):
            row_copy(step, slot, r).wait()
            return c
        lax.fori_loop(0, tg, body, 0, unroll=8)

    @pl.when(i == 0)
    def _():
        start_all(0, 0)

    for slot in range(2):
        @pl.when((i % 2 == slot) & (i + 1 < n))
        def _():
            start_all(i + 1, 1 - slot)

        @pl.when(i % 2 == slot)
        def _():
            wait_all(i, slot)
            o_ref[...] = buf[slot].astype(o_ref.dtype)


def _gather_rows(row_tok, x, tg):
    n_rows = row_tok.shape[0]
    d = x.shape[1]
    grid_spec = pltpu.PrefetchScalarGridSpec(
        num_scalar_prefetch=1,
        grid=(n_rows // tg,),
        in_specs=[pl.BlockSpec(memory_space=pl.ANY)],
        out_specs=pl.BlockSpec((tg, d), lambda i, tok: (i, 0)),
        scratch_shapes=[pltpu.VMEM((2, tg, d), x.dtype), pltpu.SemaphoreType.DMA((2,))],
    )
    return pl.pallas_call(
        functools.partial(_gather_kernel, tg=tg),
        out_shape=jax.ShapeDtypeStruct((n_rows, d), BF16),
        grid_spec=grid_spec,
        compiler_params=_cparams(("arbitrary",)),
        name="moe_gather",
    )(row_tok, x)


def _moe_ffn_kernel(te_ref, nu_ref, x_ref, wg_ref, wu_ref, wd_ref, y_ref):
    i = pl.program_id(0)
    j = pl.program_id(1)

    @pl.when(j == 0)
    def _():
        y_ref[...] = jnp.zeros_like(y_ref)

    @pl.when(i < nu_ref[0])
    def _():
        _swiglu_accumulate(y_ref, x_ref, wg_ref.at[0], wu_ref.at[0], wd_ref.at[0])


def _moe_ffn(tile_expert, n_used, xs, wg, wu, wd, tm, tf):
    n_rows, d = xs.shape
    f = wg.shape[2]
    nf = f // tf

    def jj(i, j, nu):
        return jnp.where(i < nu[0], j, nf - 1)

    grid_spec = pltpu.PrefetchScalarGridSpec(
        num_scalar_prefetch=2,
        grid=(n_rows // tm, nf),
        in_specs=[pl.BlockSpec((tm, d), lambda i, j, te, nu: (i, 0)),
                  pl.BlockSpec((1, d, tf), lambda i, j, te, nu: (te[i], 0, jj(i, j, nu))),
                  pl.BlockSpec((1, d, tf), lambda i, j, te, nu: (te[i], 0, jj(i, j, nu))),
                  pl.BlockSpec((1, tf, d), lambda i, j, te, nu: (te[i], jj(i, j, nu), 0))],
        out_specs=pl.BlockSpec((tm, d), lambda i, j, te, nu: (i, 0)),
    )
    return pl.pallas_call(
        _moe_ffn_kernel,
        out_shape=jax.ShapeDtypeStruct((n_rows, d), F32),
        grid_spec=grid_spec,
        compiler_params=_cparams(("arbitrary", "arbitrary")),
        name="moe_ffn",
    )(tile_expert, n_used, xs, wg, wu, wd)


def _combine_kernel(dest_ref, y_hbm, h_ref, gate_ref, gpost_ref, o_ref, buf, sem, *, tc):
    i = pl.program_id(0)
    n = pl.num_programs(0)

    def row_copy(step, slot, r, k):
        return pltpu.make_async_copy(y_hbm.at[pl.ds(dest_ref[(step * tc + r) * TOP_K + k], 1)],
                                     buf.at[slot, k, pl.ds(r, 1)], sem.at[slot])

    def start_all(step, slot):
        def body(r, c):
            row_copy(step, slot, r, 0).start(priority=0)
            row_copy(step, slot, r, 1).start(priority=1)
            return c
        lax.fori_loop(0, tc, body, 0, unroll=4)

    def wait_all(step, slot):
        def body(r, c):
            row_copy(step, slot, r, 0).wait()
            row_copy(step, slot, r, 1).wait()
            return c
        lax.fori_loop(0, tc, body, 0, unroll=4)

    @pl.when(i == 0)
    def _():
        start_all(0, 0)

    for slot in range(2):
        @pl.when((i % 2 == slot) & (i + 1 < n))
        def _():
            start_all(i + 1, 1 - slot)

        @pl.when(i % 2 == slot)
        def _():
            wait_all(i, slot)
            gate = gate_ref[...]
            y = buf[slot, 0] * gate[:, 0:1] + buf[slot, 1] * gate[:, 1:2]
            o_ref[...] = h_ref[...] + _rms(y, gpost_ref[...])


def _combine(dest_flat, ys, h, gates, gpost, s_real, tc):
    d = h.shape[1]
    grid_spec = pltpu.PrefetchScalarGridSpec(
        num_scalar_prefetch=1,
        grid=(s_real // tc,),
        in_specs=[pl.BlockSpec(memory_space=pl.ANY),
                  pl.BlockSpec((tc, d), lambda i, dest: (i, 0)),
                  pl.BlockSpec((tc, LANES), lambda i, dest: (i, 0)),
                  pl.BlockSpec((1, d), lambda i, dest: (0, 0))],
        out_specs=pl.BlockSpec((tc, d), lambda i, dest: (i, 0)),
        scratch_shapes=[pltpu.VMEM((2, TOP_K, tc, d), ys.dtype), pltpu.SemaphoreType.DMA((2,))],
    )
    return pl.pallas_call(
        functools.partial(_combine_kernel, tc=tc),
        out_shape=jax.ShapeDtypeStruct((s_real, d), F32),
        grid_spec=grid_spec,
        compiler_params=_cparams(("arbitrary",)),
        name="moe_combine",
    )(dest_flat, ys, h, gates, gpost)


def _rope_angles(pos, dim, theta):
    inv_freq = theta ** (-jnp.arange(0, dim, 2, dtype=F32) / dim)
    return pos.astype(F32)[:, None] * inv_freq[None, :]


def _tables(s_real, lp):
    r = jnp.arange(lp, dtype=jnp.int32)
    is_real = r < s_real
    p = r - s_real
    pos = jnp.where(is_real, r + N_META, jnp.clip(p, 0, N_META - 1))
    grid_row = jnp.where(is_real, r // GRID_W, -1)
    grid_col = jnp.where(is_real, r % GRID_W, jnp.clip(p, 0, N_META - 1))

    def cs(ang):
        c, s = jnp.cos(ang), jnp.sin(ang)
        return jnp.concatenate([c, c], axis=1), jnp.concatenate([-s, s], axis=1)

    a_c, a_s = cs(_rope_angles(pos, A_ROPE, ROPE_THETA))
    pad_c = jnp.ones((lp, LANES - A_ROPE), F32)
    acos = jnp.concatenate([a_c, pad_c], axis=1)
    asin = jnp.concatenate([a_s, 0.0 * pad_c], axis=1)
    r_c, r_s = cs(_rope_angles(grid_row, B_HEAD_DIM // 2, AXIAL_THETA))
    c_c, c_s = cs(_rope_angles(grid_col, B_HEAD_DIM // 2, AXIAL_THETA))
    bcos = jnp.concatenate([r_c, c_c], axis=1)
    bsin = jnp.concatenate([r_s, c_s], axis=1)
    w_c, w_s = cs(_rope_angles(pos, C_ROT, ROPE_THETA))
    one = jnp.ones((lp, C_HEAD_DIM - C_ROT), F32)
    ccos = jnp.concatenate([w_c, one, w_c, one], axis=1)
    csin = jnp.concatenate([w_s, 0.0 * one, w_s, 0.0 * one], axis=1)
    return acos, asin, bcos, bsin, ccos, csin


def _route(idx, tm, n_tiles):
    e_flat = idx.reshape(-1)
    n_assign = e_flat.shape[0]
    experts = jnp.arange(N_EXPERTS, dtype=jnp.int32)
    onehot = (e_flat[:, None] == experts[None, :]).astype(jnp.int32)
    csum = jnp.cumsum(onehot, axis=0)
    counts = csum[-1]
    padded = (counts + tm - 1) // tm * tm
    start = jnp.cumsum(counts) - counts
    pend = jnp.cumsum(padded)
    pstart = pend - padded
    dest_flat = jnp.sum(onehot * (pstart[None, :] + csum - 1), axis=1)
    order = jnp.argsort(e_flat * n_assign + jnp.arange(n_assign, dtype=jnp.int32))
    rows = jnp.arange(n_tiles * tm, dtype=jnp.int32)
    row_expert = jnp.minimum(jnp.searchsorted(pend, rows, side='right'), N_EXPERTS - 1)
    rank = rows - pstart[row_expert]
    assign = order[jnp.clip(start[row_expert] + rank, 0, n_assign - 1)]
    row_tok = jnp.where(rank < counts[row_expert], assign // TOP_K, rows % (n_assign // TOP_K)).astype(jnp.int32)
    tile_expert = row_expert[::tm].astype(jnp.int32)
    n_used = (pend[-1] // tm).astype(jnp.int32).reshape(1)
    return row_tok, dest_flat, tile_expert, n_used


def kernel(x, meta_tokens, even_attn_pre, even_attn_post, even_w_in, a_q_norm, a_wq_b, a_kv_norm, a_wkv_b, b_q_norm, b_k_norm, even_w_out, even_ffn_pre, even_ffn_post, ffn_w_gate, ffn_w_up, ffn_w_down, odd_attn_pre, odd_attn_post, c_w_qkv, c_b_qkv, c_sink, c_w_out, c_b_out, odd_ffn_pre, odd_ffn_post, moe_router, moe_w_gate, moe_w_up, moe_w_down):
    bsz, s_real, d = x.shape
    assert bsz == 1 and s_real % ROW_PAD == 0 and s_real % GRID_W == 0
    lp = s_real + ROW_PAD
    n_tok = s_real + N_META
    vec = lambda g: g.reshape(1, -1).astype(F32)
    acos, asin, bcos, bsin, ccos, csin = _tables(s_real, lp)
    tm_proj = _row_tile(lp, 640)

    h, xn = _prep(x[0], meta_tokens, vec(even_attn_pre[0]), lp)

    w_in = even_w_in[0]
    w_in_p = jnp.concatenate([w_in[:, :A_IN], jnp.zeros((d, LANES - A_ROPE), F32), w_in[:, A_IN:]], axis=1).astype(BF16)
    wq_p = jnp.pad(a_wq_b[0].reshape(A_Q_RANK, A_HEADS, A_NOPE + A_ROPE),
                   ((0, 0), (0, 0), (0, A_QK - A_NOPE - A_ROPE))).reshape(A_Q_RANK, A_HEADS * A_QK).astype(BF16)
    qaT, kn, vaT, krot, qbT, kb, vbT = _even_proj(
        xn, w_in_p, vec(a_q_norm[0]), wq_p, vec(a_kv_norm[0]), a_wkv_b[0].astype(BF16),
        vec(b_q_norm[0]), vec(b_k_norm[0]), acos, asin, bcos, bsin, _row_tile(lp, 256))

    tk = _key_chunk(s_real, 1024)
    oa = _mla_attention(qaT, kn, krot, vaT, s_real, _row_tile(lp, 1280), tk)
    ob = _gqa_attention(qbT, kb, vbT, s_real, _row_tile(lp, 256), tk)

    w_out = even_w_out[0].astype(BF16)
    n_a = A_HEADS * A_V
    h, xn = _out_proj0(oa, ob, w_out[:n_a], w_out[n_a:], h, vec(even_attn_post[0]), vec(even_ffn_pre[0]), tm_proj)

    tf = 512 if ffn_w_gate.shape[2] % 512 == 0 else LANES
    h, xn = _ffn0(xn, ffn_w_gate[0].astype(BF16), ffn_w_up[0].astype(BF16), ffn_w_down[0].astype(BF16),
                  h, vec(even_ffn_post[0]), vec(odd_attn_pre[0]), _row_tile(lp, 640), tf)

    nq = C_HEADS * C_HEAD_DIM
    nkv = C_KV_HEADS * C_HEAD_DIM

    def dup_heads(w):
        w4 = w.reshape(w.shape[:-1] + (C_KV_HEADS, C_HEAD_DIM))
        return jnp.concatenate([w4, w4], axis=-1).reshape(w.shape[:-1] + (C_KV_HEADS * LANES,))

    wc, bc = c_w_qkv[0], c_b_qkv[0]
    wc_p = jnp.concatenate([wc[:, :nq], dup_heads(wc[:, nq:nq + nkv]), dup_heads(wc[:, nq + nkv:])], axis=1).astype(BF16)
    bc_p = jnp.concatenate([bc[:nq], dup_heads(bc[nq:nq + nkv]), dup_heads(bc[nq + nkv:])]).reshape(1, -1)
    qc, kc, vc = _odd_proj(xn, wc_p, bc_p, ccos, csin, tm_proj)
    oc = _win_attention(qc, kc, vc, c_sink[0].astype(F32), s_real, ROW_PAD)

    h, xn_f32, idx_l, gate_l = _out_proj1(oc, c_w_out[0].astype(BF16), vec(c_b_out[0]), h, vec(odd_attn_post[0]),
                                          vec(odd_ffn_pre[0]), moe_router[0].T, tm_proj)

    tm_moe = 1024 if s_real >= 8192 else 256
    n_assign = n_tok * TOP_K
    n_tiles = -(-(n_assign + N_EXPERTS * (tm_moe - 1)) // tm_moe)
    row_tok, dest_flat, tile_expert, n_used = _route(idx_l[:n_tok, :TOP_K], tm_moe, n_tiles)
    xs = _gather_rows(row_tok, xn_f32, 1024 if tm_moe % 1024 == 0 else 256)
    tf_moe = 512 if moe_w_gate.shape[3] % 512 == 0 else LANES
    ys = _moe_ffn(tile_expert, n_used, xs, moe_w_gate[0], moe_w_up[0], moe_w_down[0], tm_moe, tf_moe)
    out = _combine(dest_flat, ys, h, gate_l, vec(odd_ffn_post[0]), s_real, 256)
    return out[None]
```

```python
import functools
import math

import jax
import jax.numpy as jnp
from jax import lax
from jax.experimental import pallas as pl
from jax.experimental.pallas import tpu as pltpu

F32 = jnp.float32
BF16 = jnp.bfloat16

N_META = 16
GRID_W = 64
RMS_EPS = 1e-6
ROPE_THETA = 500000.0
AXIAL_THETA = 10000.0
A_HEADS, A_Q_RANK, A_KV_RANK, A_NOPE, A_ROPE, A_V = 8, 512, 256, 128, 64, 128
A_IN = A_Q_RANK + A_KV_RANK + A_ROPE
B_HEADS, B_KV_HEADS, B_HEAD_DIM = 8, 2, 128
C_HEADS, C_KV_HEADS, C_HEAD_DIM, C_WINDOW = 32, 4, 64, 128
C_ROT = C_HEAD_DIM // 4
N_EXPERTS, TOP_K = 8, 2

LANES = 128
V7X_VMEM_LIMIT = 60 * 1024 * 1024
LOG2E = math.log2(math.e)

ROW_PAD = 256
ONES_ROWS = 16
A_QK = 2 * LANES


def _cparams(sem):
    return pltpu.CompilerParams(dimension_semantics=sem, vmem_limit_bytes=V7X_VMEM_LIMIT)


def _row_tile(n_rows, target):
    best = LANES
    t = LANES
    while t <= min(n_rows, target):
        if n_rows % t == 0:
            best = t
        t += LANES
    return best


def _rms(x, g):
    ms = jnp.mean(x * x, axis=-1, keepdims=True)
    return x * lax.rsqrt(ms + RMS_EPS) * g


def _dot(a, b):
    return jnp.dot(a, b, preferred_element_type=F32)


def _dot_nt(a, b):
    return lax.dot_general(a, b, (((1,), (1,)), ((), ())), preferred_element_type=F32)


def _half_swap(x, half):
    lane = lax.broadcasted_iota(jnp.int32, x.shape, x.ndim - 1)
    up = pltpu.roll(x, x.shape[-1] - half, x.ndim - 1)
    down = pltpu.roll(x, half, x.ndim - 1)
    return jnp.where((lane & half) == 0, up, down)


def _prep_kernel(x_ref, meta_ref, g_ref, h_ref, xn_ref, *, n_real_blocks):
    i = pl.program_id(0)

    @pl.when(i < n_real_blocks)
    def _():
        h = x_ref[...]
        h_ref[...] = h
        xn_ref[...] = _rms(h, g_ref[...]).astype(xn_ref.dtype)

    @pl.when(i >= n_real_blocks)
    def _():
        h_ref[...] = jnp.zeros_like(h_ref)
        xn_ref[...] = jnp.zeros_like(xn_ref)
        m = meta_ref[...]
        h_ref[0:N_META, :] = m
        xn_ref[0:N_META, :] = _rms(m, g_ref[...]).astype(xn_ref.dtype)


def _prep(x2d, meta, g, lp):
    s, d = x2d.shape
    tm = ROW_PAD
    nrb = s // tm
    return pl.pallas_call(
        functools.partial(_prep_kernel, n_real_blocks=nrb),
        out_shape=(jax.ShapeDtypeStruct((lp, d), F32), jax.ShapeDtypeStruct((lp, d), BF16)),
        grid=(lp // tm,),
        in_specs=[pl.BlockSpec((tm, d), lambda i: (jnp.minimum(i, nrb - 1), 0)),
                  pl.BlockSpec((N_META, d), lambda i: (0, 0)),
                  pl.BlockSpec((1, d), lambda i: (0, 0))],
        out_specs=(pl.BlockSpec((tm, d), lambda i: (i, 0)), pl.BlockSpec((tm, d), lambda i: (i, 0))),
        compiler_params=_cparams(("parallel",)),
        name="prep",
    )(x2d, meta, g)


def _even_proj_kernel(xn_ref, w_in_ref, gq_ref, wq_ref, gkv_ref, wkv_ref, bgq_ref, bgk_ref,
                      acos_ref, asin_ref, bcos_ref, bsin_ref,
                      qaT_ref, kn_ref, vaT_ref, krot_ref, qbT_ref, kb_ref, vbT_ref):
    z = _dot(xn_ref[...], w_in_ref[...])
    acos, asin = acos_ref[...], asin_ref[...]
    bcos, bsin = bcos_ref[...], bsin_ref[...]
    sa = LOG2E / math.sqrt(A_NOPE + A_ROPE)
    sb = LOG2E / math.sqrt(B_HEAD_DIM)
    o = 0
    cqn = _rms(z[:, o:o + A_Q_RANK], gq_ref[...]).astype(BF16)
    o += A_Q_RANK
    qa = _dot(cqn, wq_ref[...])
    for h in range(A_HEADS):
        c = h * A_QK
        qaT_ref[c:c + LANES, :] = (qa[:, c:c + LANES] * sa).T.astype(BF16)
        rot = qa[:, c + LANES:c + A_QK]
        rot = rot * acos + _half_swap(rot, A_ROPE // 2) * asin
        qaT_ref[c + LANES:c + A_QK, :] = (rot * sa).T.astype(BF16)
    ckvn = _rms(z[:, o:o + A_KV_RANK], gkv_ref[...]).astype(BF16)
    o += A_KV_RANK
    kv = _dot(ckvn, wkv_ref[...])
    for h in range(A_HEADS):
        c = h * (A_NOPE + A_V)
        kn_ref[:, h * A_NOPE:(h + 1) * A_NOPE] = kv[:, c:c + A_NOPE].astype(BF16)
        vaT_ref[h * A_V:(h + 1) * A_V, :] = kv[:, c + A_NOPE:c + A_NOPE + A_V].T.astype(BF16)
    kr = z[:, o:o + LANES]
    o += LANES
    krot_ref[...] = (kr * acos + _half_swap(kr, A_ROPE // 2) * asin).astype(BF16)

    def norm_rope(x, g):
        xn = _rms(x, g)
        return xn * bcos + _half_swap(xn, B_HEAD_DIM // 4) * bsin

    for h in range(B_HEADS):
        c = h * B_HEAD_DIM
        qbT_ref[c:c + B_HEAD_DIM, :] = (norm_rope(z[:, o + c:o + c + B_HEAD_DIM], bgq_ref[...]) * sb).T.astype(BF16)
    o += B_HEADS * B_HEAD_DIM
    for h in range(B_KV_HEADS):
        c = h * B_HEAD_DIM
        kb_ref[:, c:c + B_HEAD_DIM] = norm_rope(z[:, o + c:o + c + B_HEAD_DIM], bgk_ref[...]).astype(BF16)
    o += B_KV_HEADS * B_HEAD_DIM
    for h in range(B_KV_HEADS):
        c = h * B_HEAD_DIM
        vbT_ref[c:c + B_HEAD_DIM, :] = z[:, o + c:o + c + B_HEAD_DIM].T.astype(BF16)


def _even_proj(xn, w_in_p, gq, wq_p, gkv, wkv, bgq, bgk, acos, asin, bcos, bsin, tm):
    lp, d = xn.shape
    row = lambda n: pl.BlockSpec((tm, n), lambda i: (i, 0))
    col = lambda n: pl.BlockSpec((n, tm), lambda i: (0, i))
    full = lambda a: pl.BlockSpec(a.shape, lambda i: (0, 0))
    outs = (("t", A_HEADS * A_QK), ("n", A_HEADS * A_NOPE), ("t", A_HEADS * A_V), ("n", LANES),
            ("t", B_HEADS * B_HEAD_DIM), ("n", B_KV_HEADS * B_HEAD_DIM), ("t", B_KV_HEADS * B_HEAD_DIM))
    return pl.pallas_call(
        _even_proj_kernel,
        out_shape=tuple(jax.ShapeDtypeStruct((n, lp) if k == "t" else (lp, n), BF16) for k, n in outs),
        grid=(lp // tm,),
        in_specs=[row(d), full(w_in_p), full(gq), full(wq_p), full(gkv), full(wkv), full(bgq), full(bgk),
                  row(LANES), row(LANES), row(LANES), row(LANES)],
        out_specs=tuple(col(n) if k == "t" else row(n) for k, n in outs),
        compiler_params=_cparams(("parallel",)),
        name="even_proj",
    )(xn, w_in_p, gq, wq_p, gkv, wkv, bgq, bgk, acos, asin, bcos, bsin)


def _flash_t(qT, k_chunk, vT_chunk, k_meta, vT_meta, m_sc, acc_sc, s_sc, p_sc, mx_sc, al_sc, *, n_chunks):
    def with_ones(vT):
        return jnp.concatenate([vT, jnp.ones((ONES_ROWS, vT.shape[1]), vT.dtype)], axis=0)

    km, vTm = k_meta(), vT_meta()
    s = _dot(km, qT)
    key = lax.broadcasted_iota(jnp.int32, s.shape, 0)
    s = jnp.where(key < N_META, s, -jnp.inf)
    m = jnp.max(s, axis=0, keepdims=True)
    m_sc[...] = m
    acc_sc[...] = _dot(with_ones(vTm), jnp.exp2(s - m).astype(BF16))

    nq = qT.shape[1]

    def scores(c, b):
        s = _dot(k_chunk(c), qT)
        s_sc[b, :, :nq] = s
        mx_sc[b] = jnp.max(s, axis=0, keepdims=True)

    def probs(b):
        m_prev = m_sc[...]
        m_new = jnp.maximum(m_prev, mx_sc[b])
        al_sc[b] = jnp.exp2(m_prev - m_new)
        m_sc[...] = m_new
        p_sc[b, :, :nq] = jnp.exp2(s_sc[b, :, :nq] - m_new).astype(BF16)

    def accumulate(c, b):
        acc_sc[...] = al_sc[b] * acc_sc[...] + _dot(with_ones(vT_chunk(c)), p_sc[b, :, :nq])

    assert n_chunks % 2 == 0
    scores(0, 0)
    scores(1, 1)
    probs(0)

    def step(i, carry):
        c = 2 * i
        scores(c + 2, 0)
        probs(1)
        accumulate(c, 0)
        scores(c + 3, 1)
        probs(0)
        accumulate(c + 1, 1)
        return carry

    lax.fori_loop(0, n_chunks // 2 - 1, step, 0)
    probs(1)
    accumulate(n_chunks - 2, 0)
    accumulate(n_chunks - 1, 1)


def _flash_scratch(dv, nq, tk):
    wide = nq if (nq // LANES) % 2 else nq + LANES
    return [pltpu.VMEM((1, nq), F32), pltpu.VMEM((dv + ONES_ROWS, nq), F32),
            pltpu.VMEM((2, tk, wide), F32), pltpu.VMEM((2, tk, wide), BF16),
            pltpu.VMEM((2, 1, nq), F32), pltpu.VMEM((2, 1, nq), F32)]


def _key_chunk(s_real, target):
    best = None
    for t in range(LANES, min(s_real // 2, target) + 1, LANES):
        if s_real % (2 * t) == 0:
            best = t
    assert best is not None
    return best


def _mla_attn_kernel(qT_ref, kn_ref, kr_ref, vT_ref, knm_ref, krm_ref, vTm_ref, o_ref, *scratch, tk):
    rows = lambda c: pl.ds(pl.multiple_of(c * tk, tk), tk)
    _flash_t(qT_ref[...],
             lambda c: jnp.concatenate([kn_ref[rows(c), :], kr_ref[rows(c), :]], axis=1),
             lambda c: vT_ref[:, rows(c)],
             lambda: jnp.concatenate([knm_ref[...], krm_ref[...]], axis=1),
             lambda: vTm_ref[...],
             *scratch, n_chunks=kn_ref.shape[0] // tk)
    acc_sc = scratch[1]
    o_ref[...] = (acc_sc[:A_V, :] / acc_sc[A_V:A_V + 1, :]).T.astype(o_ref.dtype)


def _mla_attention(qaT, kn, krot, vaT, s_real, tq, tk):
    lp = qaT.shape[1]
    mb = s_real // LANES
    return pl.pallas_call(
        functools.partial(_mla_attn_kernel, tk=tk),
        out_shape=jax.ShapeDtypeStruct((lp, A_HEADS * A_V), BF16),
        grid=(A_HEADS, lp // tq),
        in_specs=[pl.BlockSpec((A_QK, tq), lambda h, i: (h, i)),
                  pl.BlockSpec((s_real, A_NOPE), lambda h, i: (0, h), pipeline_mode=pl.Buffered(1)),
                  pl.BlockSpec((s_real, LANES), lambda h, i: (0, 0), pipeline_mode=pl.Buffered(1)),
                  pl.BlockSpec((A_V, s_real), lambda h, i: (h, 0), pipeline_mode=pl.Buffered(1)),
                  pl.BlockSpec((LANES, A_NOPE), lambda h, i: (mb, h)),
                  pl.BlockSpec((LANES, LANES), lambda h, i: (mb, 0)),
                  pl.BlockSpec((A_V, LANES), lambda h, i: (h, mb))],
        out_specs=pl.BlockSpec((tq, A_V), lambda h, i: (i, h)),
        scratch_shapes=_flash_scratch(A_V, tq, tk),
        compiler_params=_cparams(("parallel", "parallel")),
        name="mla_attention",
    )(qaT, kn, krot, vaT, kn, krot, vaT)


def _gqa_attn_kernel(qT_ref, k_ref, vT_ref, km_ref, vTm_ref, o_ref, *scratch, tk, n_stack):
    dh = k_ref.shape[1]
    tq = qT_ref.shape[1]
    rows = lambda c: pl.ds(pl.multiple_of(c * tk, tk), tk)
    qT = jnp.concatenate([qT_ref[j * dh:(j + 1) * dh, :] for j in range(n_stack)], axis=1)
    _flash_t(qT, lambda c: k_ref[rows(c), :], lambda c: vT_ref[:, rows(c)], lambda: km_ref[...], lambda: vTm_ref[...],
             *scratch, n_chunks=k_ref.shape[0] // tk)
    acc_sc = scratch[1]
    for j in range(n_stack):
        sl = slice(j * tq, (j + 1) * tq)
        o_ref[:, j * dh:(j + 1) * dh] = (acc_sc[:dh, sl] / acc_sc[dh:dh + 1, sl]).T.astype(o_ref.dtype)


def _gqa_attention(qbT, kb, vbT, s_real, tq, tk):
    lp = qbT.shape[1]
    grp = B_HEADS // B_KV_HEADS
    dh = B_HEAD_DIM
    mb = s_real // LANES
    return pl.pallas_call(
        functools.partial(_gqa_attn_kernel, tk=tk, n_stack=grp),
        out_shape=jax.ShapeDtypeStruct((lp, B_HEADS * dh), BF16),
        grid=(B_KV_HEADS, lp // tq),
        in_specs=[pl.BlockSpec((grp * dh, tq), lambda g, i: (g, i)),
                  pl.BlockSpec((s_real, dh), lambda g, i: (0, g)),
                  pl.BlockSpec((dh, s_real), lambda g, i: (g, 0)),
                  pl.BlockSpec((LANES, dh), lambda g, i: (mb, g)),
                  pl.BlockSpec((dh, LANES), lambda g, i: (g, mb))],
        out_specs=pl.BlockSpec((tq, grp * dh), lambda g, i: (i, g)),
        scratch_shapes=_flash_scratch(dh, grp * tq, tk),
        compiler_params=_cparams(("parallel", "parallel")),
        name="gqa_attention",
    )(qbT, kb, vbT, kb, vbT)


def _out_proj0_kernel(oa_ref, ob_ref, wa_ref, wb_ref, h_ref, gpost_ref, gnext_ref, hn_ref, xn_ref):
    y = _dot(oa_ref[...], wa_ref[...]) + _dot(ob_ref[...], wb_ref[...])
    h = h_ref[...] + _rms(y, gpost_ref[...])
    hn_ref[...] = h
    xn_ref[...] = _rms(h, gnext_ref[...]).astype(xn_ref.dtype)


def _out_proj0(oa, ob, wa, wb, h, gpost, gnext, tm):
    lp, d = h.shape
    row = lambda n: pl.BlockSpec((tm, n), lambda i: (i, 0))
    full = lambda a: pl.BlockSpec(a.shape, lambda i: (0, 0))
    return pl.pallas_call(
        _out_proj0_kernel,
        out_shape=(jax.ShapeDtypeStruct((lp, d), F32), jax.ShapeDtypeStruct((lp, d), BF16)),
        grid=(lp // tm,),
        in_specs=[row(oa.shape[1]), row(ob.shape[1]), full(wa), full(wb), row(d), full(gpost), full(gnext)],
        out_specs=(row(d), row(d)),
        compiler_params=_cparams(("parallel",)),
        name="out_proj0",
    )(oa, ob, wa, wb, h, gpost, gnext)


def _swiglu_accumulate(y_ref, x_ref, wg_ref, wu_ref, wd_ref):
    x = x_ref[...]
    g = _dot(x, wg_ref[...].astype(BF16))
    u = _dot(x, wu_ref[...].astype(BF16))
    a = (g * jax.nn.sigmoid(g) * u).astype(BF16)
    y_ref[...] += _dot(a, wd_ref[...].astype(BF16))


def _ffn0_kernel(x_ref, wg_ref, wu_ref, wd_ref, h_ref, gpost_ref, gnext_ref, hn_ref, xn_ref):
    j = pl.program_id(1)

    @pl.when(j == 0)
    def _():
        hn_ref[...] = jnp.zeros_like(hn_ref)

    _swiglu_accumulate(hn_ref, x_ref, wg_ref, wu_ref, wd_ref)

    @pl.when(j == pl.num_programs(1) - 1)
    def _():
        h = h_ref[...] + _rms(hn_ref[...], gpost_ref[...])
        hn_ref[...] = h
        xn_ref[...] = _rms(h, gnext_ref[...]).astype(xn_ref.dtype)


def _ffn0(xn, wg, wu, wd, h, gpost, gnext, tm, tf):
    lp, d = h.shape
    f = wg.shape[1]
    row = pl.BlockSpec((tm, d), lambda i, j: (i, 0))
    vec = pl.BlockSpec((1, d), lambda i, j: (0, 0))
    return pl.pallas_call(
        _ffn0_kernel,
        out_shape=(jax.ShapeDtypeStruct((lp, d), F32), jax.ShapeDtypeStruct((lp, d), BF16)),
        grid=(lp // tm, f // tf),
        in_specs=[row, pl.BlockSpec((d, tf), lambda i, j: (0, j)), pl.BlockSpec((d, tf), lambda i, j: (0, j)),
                  pl.BlockSpec((tf, d), lambda i, j: (j, 0)), row, vec, vec],
        out_specs=(row, row),
        compiler_params=_cparams(("parallel", "arbitrary")),
        name="ffn0",
    )(xn, wg, wu, wd, h, gpost, gnext)


def _odd_proj_kernel(xn_ref, w_ref, b_ref, cos_ref, sin_ref, q_ref, k_ref, v_ref):
    z = _dot(xn_ref[...], w_ref[...]) + b_ref[...]
    cos, sin = cos_ref[...], sin_ref[...]
    sc = LOG2E / math.sqrt(C_HEAD_DIM)
    nq = q_ref.shape[1]
    nk = k_ref.shape[1]

    def rope(x):
        return x * cos + _half_swap(x, C_ROT // 2) * sin

    for c in range(0, nq, LANES):
        q_ref[:, c:c + LANES] = (rope(z[:, c:c + LANES]) * sc).astype(BF16)
    for c in range(0, nk, LANES):
        k_ref[:, c:c + LANES] = rope(z[:, nq + c:nq + c + LANES]).astype(BF16)
    v_ref[...] = z[:, nq + nk:].astype(BF16)


def _odd_proj(xn, w_p, b_p, cos, sin, tm):
    lp, d = xn.shape
    nq = C_HEADS * C_HEAD_DIM
    nk = C_KV_HEADS * LANES
    row = lambda n: pl.BlockSpec((tm, n), lambda i: (i, 0))
    full = lambda a: pl.BlockSpec(a.shape, lambda i: (0, 0))
    return pl.pallas_call(
        _odd_proj_kernel,
        out_shape=(jax.ShapeDtypeStruct((lp, nq), BF16), jax.ShapeDtypeStruct((lp, nk), BF16),
                   jax.ShapeDtypeStruct((lp, nk), BF16)),
        grid=(lp // tm,),
        in_specs=[row(d), full(w_p), full(b_p), row(LANES), row(LANES)],
        out_specs=(row(nq), row(nk), row(nk)),
        compiler_params=_cparams(("parallel",)),
        name="odd_proj",
    )(xn, w_p, b_p, cos, sin)


def _win_attn_kernel(sink_ref, q_ref, k0, k1, k2, k3, km, v0, v1, v2, v3, vm, o_ref, *, s_real, tq):
    g = pl.program_id(0)
    i = pl.program_id(1)
    is_meta = i == pl.num_programs(1) - 1
    nband = 4 * LANES
    kcat = jnp.concatenate([k0[...], k1[...], k2[...], k3[...], km[...]], axis=0)
    vcat = jnp.concatenate([v0[...], v1[...], v2[...], v3[...], vm[...]], axis=0)
    ncol = kcat.shape[0]
    a = lax.broadcasted_iota(jnp.int32, (tq, ncol), 0)
    c = lax.broadcasted_iota(jnp.int32, (tq, ncol), 1)
    kidx = jnp.where(is_meta, 0, i * tq - LANES) + c
    qidx = i * tq + a
    ninf = jnp.float32(-jnp.inf)
    b_real = jnp.where(jnp.abs(qidx - kidx) <= C_WINDOW, 0.0, ninf)
    b_real = jnp.where(kidx >= 0, b_real, ninf)
    b_real = jnp.where(kidx < s_real, b_real, ninf)
    b_meta_q = jnp.where(kidx <= (C_WINDOW - N_META) + a, 0.0, ninf)
    b_meta_k = jnp.where(c - nband < N_META, 0.0, ninf)
    bias = jnp.where(c < nband, jnp.where(is_meta, b_meta_q, b_real), b_meta_k)
    lane = lax.broadcasted_iota(jnp.int32, (tq, LANES), 1)
    heads_per_group = C_HEADS // C_KV_HEADS
    for pr in range(heads_per_group // 2):
        qp = q_ref[:, pr * LANES:(pr + 1) * LANES]
        halves = []
        for half in range(2):
            sink = sink_ref[g * heads_per_group + 2 * pr + half] * LOG2E
            qm = jnp.where((lane >= C_HEAD_DIM) == (half == 1), qp, jnp.zeros_like(qp))
            s = _dot_nt(qm, kcat) + bias
            m = jnp.maximum(jnp.max(s, axis=1, keepdims=True), sink)
            e = jnp.exp2(s - m)
            denom = jnp.sum(e, axis=1, keepdims=True) + jnp.exp2(sink - m)
            halves.append(_dot(e.astype(BF16), vcat) / denom)
        o_ref[:, pr * LANES:(pr + 1) * LANES] = jnp.where(lane < C_HEAD_DIM, halves[0], halves[1]).astype(o_ref.dtype)


def _win_attention(qc, kc, vc, sink, s_real, tq):
    lp = qc.shape[0]
    nq = lp // tq
    nkb = s_real // LANES
    per = tq // LANES
    gw = (C_HEADS // C_KV_HEADS) * C_HEAD_DIM

    def band(k):
        def index(g, i, sink_ref):
            b = jnp.where(i == nq - 1, k, jnp.clip(i * per - 1 + k, 0, nkb - 1))
            return (b, g)
        return pl.BlockSpec((LANES, LANES), index)

    meta = pl.BlockSpec((LANES, LANES), lambda g, i, sink_ref: (nkb, g))
    grid_spec = pltpu.PrefetchScalarGridSpec(
        num_scalar_prefetch=1,
        grid=(C_KV_HEADS, nq),
        in_specs=[pl.BlockSpec((tq, gw), lambda g, i, sink_ref: (i, g))]
                 + [band(k) for k in range(4)] + [meta] + [band(k) for k in range(4)] + [meta],
        out_specs=pl.BlockSpec((tq, gw), lambda g, i, sink_ref: (i, g)),
    )
    return pl.pallas_call(
        functools.partial(_win_attn_kernel, s_real=s_real, tq=tq),
        out_shape=jax.ShapeDtypeStruct((lp, C_HEADS * C_HEAD_DIM), BF16),
        grid_spec=grid_spec,
        compiler_params=_cparams(("parallel", "parallel")),
        name="win_attention",
    )(sink, qc, kc, kc, kc, kc, kc, vc, vc, vc, vc, vc)


def _out_proj1_kernel(o_ref, w_ref, b_ref, h_ref, gpost_ref, gnext_ref, router_ref, hn_ref, xn_ref, idx_ref, gate_ref):
    y = _dot(o_ref[...], w_ref[...]) + b_ref[...]
    h = h_ref[...] + _rms(y, gpost_ref[...])
    hn_ref[...] = h
    xn = _rms(h, gnext_ref[...])
    xn_ref[...] = xn
    logit = [jnp.sum(xn * router_ref[e:e + 1, :], axis=1, keepdims=True) for e in range(N_EXPERTS)]

    def top(cols):
        v = functools.reduce(jnp.maximum, cols)
        i = jnp.full_like(v, N_EXPERTS, dtype=jnp.int32)
        for e in reversed(range(N_EXPERTS)):
            i = jnp.where(cols[e] == v, e, i)
        return v, i

    v1, i1 = top(logit)
    v2, i2 = top([jnp.where(i1 == e, -jnp.inf, logit[e]) for e in range(N_EXPERTS)])
    e2 = jnp.exp(v2 - v1)
    g1 = 1.0 / (1.0 + e2)
    g2 = e2 / (1.0 + e2)
    lane = lax.broadcasted_iota(jnp.int32, idx_ref.shape, 1)
    idx_ref[...] = jnp.where(lane == 0, i1, jnp.where(lane == 1, i2, 0))
    gate_ref[...] = jnp.where(lane == 0, g1, jnp.where(lane == 1, g2, 0.0))


def _out_proj1(o, w, b, h, gpost, gnext, router_p, tm):
    lp, d = h.shape
    row = lambda n: pl.BlockSpec((tm, n), lambda i: (i, 0))
    full = lambda a: pl.BlockSpec(a.shape, lambda i: (0, 0))
    return pl.pallas_call(
        _out_proj1_kernel,
        out_shape=(jax.ShapeDtypeStruct((lp, d), F32), jax.ShapeDtypeStruct((lp, d), F32),
                   jax.ShapeDtypeStruct((lp, LANES), jnp.int32), jax.ShapeDtypeStruct((lp, LANES), F32)),
        grid=(lp // tm,),
        in_specs=[row(o.shape[1]), full(w), full(b), row(d), full(gpost), full(gnext), full(router_p)],
        out_specs=(row(d), row(d), row(LANES), row(LANES)),
        compiler_params=_cparams(("parallel",)),
        name="out_proj1",
    )(o, w, b, h, gpost, gnext, router_p)


def _gather_kernel(tok_ref, x_hbm, o_ref, buf, sem, *, tg):
    i = pl.program_id(0)
    n = pl.num_programs(0)

    def row_copy(step, slot, r):
        return pltpu.make_async_copy(x_hbm.at[pl.ds(tok_ref[step * tg + r], 1)], buf.at[slot, pl.ds(r, 1)],
                                     sem.at[slot])

    def start_all(step, slot):
        def body(r2, c):
            row_copy(step, slot, 2 * r2).start(priority=0)
            row_copy(step, slot, 2 * r2 + 1).start(priority=1)
            return c
        lax.fori_loop(0, tg // 2, body, 0, unroll=4)

    def wait_all(step, slot):
        def body(r, c):
            row_copy(step, slot, r).wait()
            return c
        lax.fori_loop(0, tg, body, 0, unroll=8)

    @pl.when(i == 0)
    def _():
        start_all(0, 0)

    for slot in range(2):
        @pl.when((i % 2 == slot) & (i + 1 < n))
        def _():
            start_all(i + 1, 1 - slot)

        @pl.when(i % 2 == slot)
        def _():
            wait_all(i, slot)
            o_ref[...] = buf[slot].astype(o_ref.dtype)


def _gather_rows(row_tok, x, tg):
    n_rows = row_tok.shape[0]
    d = x.shape[1]
    grid_spec = pltpu.PrefetchScalarGridSpec(
        num_scalar_prefetch=1,
        grid=(n_rows // tg,),
        in_specs=[pl.BlockSpec(memory_space=pl.ANY)],
        out_specs=pl.BlockSpec((tg, d), lambda i, tok: (i, 0)),
        scratch_shapes=[pltpu.VMEM((2, tg, d), x.dtype), pltpu.SemaphoreType.DMA((2,))],
    )
    return pl.pallas_call(
        functools.partial(_gather_kernel, tg=tg),
        out_shape=jax.ShapeDtypeStruct((n_rows, d), BF16),
        grid_spec=grid_spec,
        compiler_params=_cparams(("arbitrary",)),
        name="moe_gather",
    )(row_tok, x)


def _moe_ffn_kernel(te_ref, nu_ref, x_ref, wg_ref, wu_ref, wd_ref, y_ref):
    i = pl.program_id(0)
    j = pl.program_id(1)

    @pl.when(j == 0)
    def _():
        y_ref[...] = jnp.zeros_like(y_ref)

    @pl.when(i < nu_ref[0])
    def _():
        _swiglu_accumulate(y_ref, x_ref, wg_ref.at[0], wu_ref.at[0], wd_ref.at[0])


def _moe_ffn(tile_expert, n_used, xs, wg, wu, wd, tm, tf):
    n_rows, d = xs.shape
    f = wg.shape[2]
    nf = f // tf

    def jj(i, j, nu):
        return jnp.where(i < nu[0], j, nf - 1)

    grid_spec = pltpu.PrefetchScalarGridSpec(
        num_scalar_prefetch=2,
        grid=(n_rows // tm, nf),
        in_specs=[pl.BlockSpec((tm, d), lambda i, j, te, nu: (i, 0)),
                  pl.BlockSpec((1, d, tf), lambda i, j, te, nu: (te[i], 0, jj(i, j, nu))),
                  pl.BlockSpec((1, d, tf), lambda i, j, te, nu: (te[i], 0, jj(i, j, nu))),
                  pl.BlockSpec((1, tf, d), lambda i, j, te, nu: (te[i], jj(i, j, nu), 0))],
        out_specs=pl.BlockSpec((tm, d), lambda i, j, te, nu: (i, 0)),
    )
    return pl.pallas_call(
        _moe_ffn_kernel,
        out_shape=jax.ShapeDtypeStruct((n_rows, d), F32),
        grid_spec=grid_spec,
        compiler_params=_cparams(("arbitrary", "arbitrary")),
        name="moe_ffn",
    )(tile_expert, n_used, xs, wg, wu, wd)


def _combine_kernel(dest_ref, y_hbm, h_ref, gate_ref, gpost_ref, o_ref, buf, sem, *, tc):
    i = pl.program_id(0)
    n = pl.num_programs(0)

    def row_copy(step, slot, r, k):
        return pltpu.make_async_copy(y_hbm.at[pl.ds(dest_ref[(step * tc + r) * TOP_K + k], 1)],
                                     buf.at[slot, k, pl.ds(r, 1)], sem.at[slot])

    def start_all(step, slot):
        def body(r, c):
            row_copy(step, slot, r, 0).start(priority=0)
            row_copy(step, slot, r, 1).start(priority=1)
            return c
        lax.fori_loop(0, tc, body, 0, unroll=4)

    def wait_all(step, slot):
        def body(r, c):
            row_copy(step, slot, r, 0).wait()
            row_copy(step, slot, r, 1).wait()
            return c
        lax.fori_loop(0, tc, body, 0, unroll=4)

    @pl.when(i == 0)
    def _():
        start_all(0, 0)

    for slot in range(2):
        @pl.when((i % 2 == slot) & (i + 1 < n))
        def _():
            start_all(i + 1, 1 - slot)

        @pl.when(i % 2 == slot)
        def _():
            wait_all(i, slot)
            gate = gate_ref[...]
            y = buf[slot, 0] * gate[:, 0:1] + buf[slot, 1] * gate[:, 1:2]
            o_ref[...] = h_ref[...] + _rms(y, gpost_ref[...])


def _combine(dest_flat, ys, h, gates, gpost, s_real, tc):
    d = h.shape[1]
    grid_spec = pltpu.PrefetchScalarGridSpec(
        num_scalar_prefetch=1,
        grid=(s_real // tc,),
        in_specs=[pl.BlockSpec(memory_space=pl.ANY),
                  pl.BlockSpec((tc, d), lambda i, dest: (i, 0)),
                  pl.BlockSpec((tc, LANES), lambda i, dest: (i, 0)),
                  pl.BlockSpec((1, d), lambda i, dest: (0, 0))],
        out_specs=pl.BlockSpec((tc, d), lambda i, dest: (i, 0)),
        scratch_shapes=[pltpu.VMEM((2, TOP_K, tc, d), ys.dtype), pltpu.SemaphoreType.DMA((2,))],
    )
    return pl.pallas_call(
        functools.partial(_combine_kernel, tc=tc),
        out_shape=jax.ShapeDtypeStruct((s_real, d), F32),
        grid_spec=grid_spec,
        compiler_params=_cparams(("arbitrary",)),
        name="moe_combine",
    )(dest_flat, ys, h, gates, gpost)


def _rope_angles(pos, dim, theta):
    inv_freq = theta ** (-jnp.arange(0, dim, 2, dtype=F32) / dim)
    return pos.astype(F32)[:, None] * inv_freq[None, :]


def _tables(s_real, lp):
    r = jnp.arange(lp, dtype=jnp.int32)
    is_real = r < s_real
    p = r - s_real
    pos = jnp.where(is_real, r + N_META, jnp.clip(p, 0, N_META - 1))
    grid_row = jnp.where(is_real, r // GRID_W, -1)
    grid_col = jnp.where(is_real, r % GRID_W, jnp.clip(p, 0, N_META - 1))

    def cs(ang):
        c, s = jnp.cos(ang), jnp.sin(ang)
        return jnp.concatenate([c, c], axis=1), jnp.concatenate([-s, s], axis=1)

    a_c, a_s = cs(_rope_angles(pos, A_ROPE, ROPE_THETA))
    pad_c = jnp.ones((lp, LANES - A_ROPE), F32)
    acos = jnp.concatenate([a_c, pad_c], axis=1)
    asin = jnp.concatenate([a_s, 0.0 * pad_c], axis=1)
    r_c, r_s = cs(_rope_angles(grid_row, B_HEAD_DIM // 2, AXIAL_THETA))
    c_c, c_s = cs(_rope_angles(grid_col, B_HEAD_DIM // 2, AXIAL_THETA))
    bcos = jnp.concatenate([r_c, c_c], axis=1)
    bsin = jnp.concatenate([r_s, c_s], axis=1)
    w_c, w_s = cs(_rope_angles(pos, C_ROT, ROPE_THETA))
    one = jnp.ones((lp, C_HEAD_DIM - C_ROT), F32)
    ccos = jnp.concatenate([w_c, one, w_c, one], axis=1)
    csin = jnp.concatenate([w_s, 0.0 * one, w_s, 0.0 * one], axis=1)
    return acos, asin, bcos, bsin, ccos, csin


def _route(idx, tm, n_tiles):
    e_flat = idx.reshape(-1)
    n_assign = e_flat.shape[0]
    experts = jnp.arange(N_EXPERTS, dtype=jnp.int32)
    onehot = (e_flat[:, None] == experts[None, :]).astype(jnp.int32)
    csum = jnp.cumsum(onehot, axis=0)
    counts = csum[-1]
    padded = (counts + tm - 1) // tm * tm
    start = jnp.cumsum(counts) - counts
    pend = jnp.cumsum(padded)
    pstart = pend - padded
    dest_flat = jnp.sum(onehot * (pstart[None, :] + csum - 1), axis=1)
    order = jnp.argsort(e_flat * n_assign + jnp.arange(n_assign, dtype=jnp.int32))
    rows = jnp.arange(n_tiles * tm, dtype=jnp.int32)
    row_expert = jnp.minimum(jnp.searchsorted(pend, rows, side='right'), N_EXPERTS - 1)
    rank = rows - pstart[row_expert]
    assign = order[jnp.clip(start[row_expert] + rank, 0, n_assign - 1)]
    row_tok = jnp.where(rank < counts[row_expert], assign // TOP_K, rows % (n_assign // TOP_K)).astype(jnp.int32)
    tile_expert = row_expert[::tm].astype(jnp.int32)
    n_used = (pend[-1] // tm).astype(jnp.int32).reshape(1)
    return row_tok, dest_flat, tile_expert, n_used


def kernel(x, meta_tokens, even_attn_pre, even_attn_post, even_w_in, a_q_norm, a_wq_b, a_kv_norm, a_wkv_b, b_q_norm, b_k_norm, even_w_out, even_ffn_pre, even_ffn_post, ffn_w_gate, ffn_w_up, ffn_w_down, odd_attn_pre, odd_attn_post, c_w_qkv, c_b_qkv, c_sink, c_w_out, c_b_out, odd_ffn_pre, odd_ffn_post, moe_router, moe_w_gate, moe_w_up, moe_w_down):
    bsz, s_real, d = x.shape
    assert bsz == 1 and s_real % ROW_PAD == 0 and s_real % GRID_W == 0
    lp = s_real + ROW_PAD
    n_tok = s_real + N_META
    vec = lambda g: g.reshape(1, -1).astype(F32)
    acos, asin, bcos, bsin, ccos, csin = _tables(s_real, lp)
    tm_proj = _row_tile(lp, 640)

    h, xn = _prep(x[0], meta_tokens, vec(even_attn_pre[0]), lp)

    w_in = even_w_in[0]
    w_in_p = jnp.concatenate([w_in[:, :A_IN], jnp.zeros((d, LANES - A_ROPE), F32), w_in[:, A_IN:]], axis=1).astype(BF16)
    wq_p = jnp.pad(a_wq_b[0].reshape(A_Q_RANK, A_HEADS, A_NOPE + A_ROPE),
                   ((0, 0), (0, 0), (0, A_QK - A_NOPE - A_ROPE))).reshape(A_Q_RANK, A_HEADS * A_QK).astype(BF16)
    qaT, kn, vaT, krot, qbT, kb, vbT = _even_proj(
        xn, w_in_p, vec(a_q_norm[0]), wq_p, vec(a_kv_norm[0]), a_wkv_b[0].astype(BF16),
        vec(b_q_norm[0]), vec(b_k_norm[0]), acos, asin, bcos, bsin, _row_tile(lp, 256))

    tk = _key_chunk(s_real, 2048)
    oa = _mla_attention(qaT, kn, krot, vaT, s_real, _row_tile(lp, 1280), tk)
    ob = _gqa_attention(qbT, kb, vbT, s_real, _row_tile(lp, 256), tk)

    w_out = even_w_out[0].astype(BF16)
    n_a = A_HEADS * A_V
    h, xn = _out_proj0(oa, ob, w_out[:n_a], w_out[n_a:], h, vec(even_attn_post[0]), vec(even_ffn_pre[0]), tm_proj)

    tf = 512 if ffn_w_gate.shape[2] % 512 == 0 else LANES
    h, xn = _ffn0(xn, ffn_w_gate[0].astype(BF16), ffn_w_up[0].astype(BF16), ffn_w_down[0].astype(BF16),
                  h, vec(even_ffn_post[0]), vec(odd_attn_pre[0]), _row_tile(lp, 640), tf)

    nq = C_HEADS * C_HEAD_DIM
    nkv = C_KV_HEADS * C_HEAD_DIM

    def dup_heads(w):
        w4 = w.reshape(w.shape[:-1] + (C_KV_HEADS, C_HEAD_DIM))
        return jnp.concatenate([w4, w4], axis=-1).reshape(w.shape[:-1] + (C_KV_HEADS * LANES,))

    wc, bc = c_w_qkv[0], c_b_qkv[0]
    wc_p = jnp.concatenate([wc[:, :nq], dup_heads(wc[:, nq:nq + nkv]), dup_heads(wc[:, nq + nkv:])], axis=1).astype(BF16)
    bc_p = jnp.concatenate([bc[:nq], dup_heads(bc[nq:nq + nkv]), dup_heads(bc[nq + nkv:])]).reshape(1, -1)
    qc, kc, vc = _odd_proj(xn, wc_p, bc_p, ccos, csin, tm_proj)
    oc = _win_attention(qc, kc, vc, c_sink[0].astype(F32), s_real, ROW_PAD)

    h, xn_f32, idx_l, gate_l = _out_proj1(oc, c_w_out[0].astype(BF16), vec(c_b_out[0]), h, vec(odd_attn_post[0]),
                                          vec(odd_ffn_pre[0]), moe_router[0].T, tm_proj)

    tm_moe = 1024 if s_real >= 8192 else 256
    n_assign = n_tok * TOP_K
    n_tiles = -(-(n_assign + N_EXPERTS * (tm_moe - 1)) // tm_moe)
    row_tok, dest_flat, tile_expert, n_used = _route(idx_l[:n_tok, :TOP_K], tm_moe, n_tiles)
    xs = _gather_rows(row_tok, xn_f32, 1024 if tm_moe % 1024 == 0 else 256)
    tf_moe = 512 if moe_w_gate.shape[3] % 512 == 0 else LANES
    ys = _moe_ffn(tile_expert, n_used, xs, moe_w_gate[0], moe_w_up[0], moe_w_down[0], tm_moe, tf_moe)
    out = _combine(dest_flat, ys, h, gate_l, vec(odd_ffn_post[0]), s_real, 256)
    return out[None]
```

```python
import functools
import math

import jax
import jax.numpy as jnp
from jax import lax
from jax.experimental import pallas as pl
from jax.experimental.pallas import tpu as pltpu

F32 = jnp.float32
BF16 = jnp.bfloat16

N_META = 16
GRID_W = 64
RMS_EPS = 1e-6
ROPE_THETA = 500000.0
AXIAL_THETA = 10000.0
A_HEADS, A_Q_RANK, A_KV_RANK, A_NOPE, A_ROPE, A_V = 8, 512, 256, 128, 64, 128
A_IN = A_Q_RANK + A_KV_RANK + A_ROPE
B_HEADS, B_KV_HEADS, B_HEAD_DIM = 8, 2, 128
C_HEADS, C_KV_HEADS, C_HEAD_DIM, C_WINDOW = 32, 4, 64, 128
C_ROT = C_HEAD_DIM // 4
N_EXPERTS, TOP_K = 8, 2

LANES = 128
V7X_VMEM_LIMIT = 60 * 1024 * 1024
LOG2E = math.log2(math.e)

ROW_PAD = 256
ONES_ROWS = 16
A_QK = 2 * LANES


def _cparams(sem):
    return pltpu.CompilerParams(dimension_semantics=sem, vmem_limit_bytes=V7X_VMEM_LIMIT)


def _row_tile(n_rows, target):
    best = LANES
    t = LANES
    while t <= min(n_rows, target):
        if n_rows % t == 0:
            best = t
        t += LANES
    return best


def _rms(x, g):
    ms = jnp.mean(x * x, axis=-1, keepdims=True)
    return x * lax.rsqrt(ms + RMS_EPS) * g


def _dot(a, b):
    return jnp.dot(a, b, preferred_element_type=F32)


def _dot_nt(a, b):
    return lax.dot_general(a, b, (((1,), (1,)), ((), ())), preferred_element_type=F32)


def _half_swap(x, half):
    lane = lax.broadcasted_iota(jnp.int32, x.shape, x.ndim - 1)
    up = pltpu.roll(x, x.shape[-1] - half, x.ndim - 1)
    down = pltpu.roll(x, half, x.ndim - 1)
    return jnp.where((lane & half) == 0, up, down)


def _prep_kernel(x_ref, meta_ref, g_ref, h_ref, xn_ref, *, n_real_blocks):
    i = pl.program_id(0)

    @pl.when(i < n_real_blocks)
    def _():
        h = x_ref[...]
        h_ref[...] = h
        xn_ref[...] = _rms(h, g_ref[...]).astype(xn_ref.dtype)

    @pl.when(i >= n_real_blocks)
    def _():
        h_ref[...] = jnp.zeros_like(h_ref)
        xn_ref[...] = jnp.zeros_like(xn_ref)
        m = meta_ref[...]
        h_ref[0:N_META, :] = m
        xn_ref[0:N_META, :] = _rms(m, g_ref[...]).astype(xn_ref.dtype)


def _prep(x2d, meta, g, lp):
    s, d = x2d.shape
    tm = ROW_PAD
    nrb = s // tm
    return pl.pallas_call(
        functools.partial(_prep_kernel, n_real_blocks=nrb),
        out_shape=(jax.ShapeDtypeStruct((lp, d), F32), jax.ShapeDtypeStruct((lp, d), BF16)),
        grid=(lp // tm,),
        in_specs=[pl.BlockSpec((tm, d), lambda i: (jnp.minimum(i, nrb - 1), 0)),
                  pl.BlockSpec((N_META, d), lambda i: (0, 0)),
                  pl.BlockSpec((1, d), lambda i: (0, 0))],
        out_specs=(pl.BlockSpec((tm, d), lambda i: (i, 0)), pl.BlockSpec((tm, d), lambda i: (i, 0))),
        compiler_params=_cparams(("parallel",)),
        name="prep",
    )(x2d, meta, g)


def _even_proj_kernel(xn_ref, w_in_ref, gq_ref, wq_ref, gkv_ref, wkv_ref, bgq_ref, bgk_ref,
                      acos_ref, asin_ref, bcos_ref, bsin_ref,
                      qaT_ref, kn_ref, vaT_ref, krot_ref, qbT_ref, kb_ref, vbT_ref):
    z = _dot(xn_ref[...], w_in_ref[...])
    acos, asin = acos_ref[...], asin_ref[...]
    bcos, bsin = bcos_ref[...], bsin_ref[...]
    sa = LOG2E / math.sqrt(A_NOPE + A_ROPE)
    sb = LOG2E / math.sqrt(B_HEAD_DIM)
    o = 0
    cqn = _rms(z[:, o:o + A_Q_RANK], gq_ref[...]).astype(BF16)
    o += A_Q_RANK
    qa = _dot(cqn, wq_ref[...])
    for h in range(A_HEADS):
        c = h * A_QK
        qaT_ref[c:c + LANES, :] = (qa[:, c:c + LANES] * sa).T.astype(BF16)
        rot = qa[:, c + LANES:c + A_QK]
        rot = rot * acos + _half_swap(rot, A_ROPE // 2) * asin
        qaT_ref[c + LANES:c + A_QK, :] = (rot * sa).T.astype(BF16)
    ckvn = _rms(z[:, o:o + A_KV_RANK], gkv_ref[...]).astype(BF16)
    o += A_KV_RANK
    kv = _dot(ckvn, wkv_ref[...])
    for h in range(A_HEADS):
        c = h * (A_NOPE + A_V)
        kn_ref[:, h * A_NOPE:(h + 1) * A_NOPE] = kv[:, c:c + A_NOPE].astype(BF16)
        vaT_ref[h * A_V:(h + 1) * A_V, :] = kv[:, c + A_NOPE:c + A_NOPE + A_V].T.astype(BF16)
    kr = z[:, o:o + LANES]
    o += LANES
    krot_ref[...] = (kr * acos + _half_swap(kr, A_ROPE // 2) * asin).astype(BF16)

    def norm_rope(x, g):
        xn = _rms(x, g)
        return xn * bcos + _half_swap(xn, B_HEAD_DIM // 4) * bsin

    for h in range(B_HEADS):
        c = h * B_HEAD_DIM
        qbT_ref[c:c + B_HEAD_DIM, :] = (norm_rope(z[:, o + c:o + c + B_HEAD_DIM], bgq_ref[...]) * sb).T.astype(BF16)
    o += B_HEADS * B_HEAD_DIM
    for h in range(B_KV_HEADS):
        c = h * B_HEAD_DIM
        kb_ref[:, c:c + B_HEAD_DIM] = norm_rope(z[:, o + c:o + c + B_HEAD_DIM], bgk_ref[...]).astype(BF16)
    o += B_KV_HEADS * B_HEAD_DIM
    for h in range(B_KV_HEADS):
        c = h * B_HEAD_DIM
        vbT_ref[c:c + B_HEAD_DIM, :] = z[:, o + c:o + c + B_HEAD_DIM].T.astype(BF16)


def _even_proj(xn, w_in_p, gq, wq_p, gkv, wkv, bgq, bgk, acos, asin, bcos, bsin, tm):
    lp, d = xn.shape
    row = lambda n: pl.BlockSpec((tm, n), lambda i: (i, 0))
    col = lambda n: pl.BlockSpec((n, tm), lambda i: (0, i))
    full = lambda a: pl.BlockSpec(a.shape, lambda i: (0, 0))
    outs = (("t", A_HEADS * A_QK), ("n", A_HEADS * A_NOPE), ("t", A_HEADS * A_V), ("n", LANES),
            ("t", B_HEADS * B_HEAD_DIM), ("n", B_KV_HEADS * B_HEAD_DIM), ("t", B_KV_HEADS * B_HEAD_DIM))
    return pl.pallas_call(
        _even_proj_kernel,
        out_shape=tuple(jax.ShapeDtypeStruct((n, lp) if k == "t" else (lp, n), BF16) for k, n in outs),
        grid=(lp // tm,),
        in_specs=[row(d), full(w_in_p), full(gq), full(wq_p), full(gkv), full(wkv), full(bgq), full(bgk),
                  row(LANES), row(LANES), row(LANES), row(LANES)],
        out_specs=tuple(col(n) if k == "t" else row(n) for k, n in outs),
        compiler_params=_cparams(("parallel",)),
        name="even_proj",
    )(xn, w_in_p, gq, wq_p, gkv, wkv, bgq, bgk, acos, asin, bcos, bsin)


def _flash_t(qT, k_chunk, vT_chunk, k_meta, vT_meta, m_sc, acc_sc, s_sc, p_sc, mx_sc, al_sc, *, n_chunks):
    def with_ones(vT):
        return jnp.concatenate([vT, jnp.ones((ONES_ROWS, vT.shape[1]), vT.dtype)], axis=0)

    km, vTm = k_meta(), vT_meta()
    s = _dot(km, qT)
    key = lax.broadcasted_iota(jnp.int32, s.shape, 0)
    s = jnp.where(key < N_META, s, -jnp.inf)
    m = jnp.max(s, axis=0, keepdims=True)
    m_sc[...] = m
    acc_sc[...] = _dot(with_ones(vTm), jnp.exp2(s - m).astype(BF16))

    nq = qT.shape[1]

    def scores(c, b):
        s = _dot(k_chunk(c), qT)
        s_sc[b, :, :nq] = s
        mx_sc[b] = jnp.max(s, axis=0, keepdims=True)

    def probs(b):
        m_prev = m_sc[...]
        m_new = jnp.maximum(m_prev, mx_sc[b])
        al_sc[b] = jnp.exp2(m_prev - m_new)
        m_sc[...] = m_new
        p_sc[b, :, :nq] = jnp.exp2(s_sc[b, :, :nq] - m_new).astype(BF16)

    def accumulate(c, b):
        acc_sc[...] = al_sc[b] * acc_sc[...] + _dot(with_ones(vT_chunk(c)), p_sc[b, :, :nq])

    assert n_chunks % 2 == 0
    scores(0, 0)
    scores(1, 1)
    probs(0)

    def step(i, carry):
        c = 2 * i
        scores(c + 2, 0)
        probs(1)
        accumulate(c, 0)
        scores(c + 3, 1)
        probs(0)
        accumulate(c + 1, 1)
        return carry

    lax.fori_loop(0, n_chunks // 2 - 1, step, 0)
    probs(1)
    accumulate(n_chunks - 2, 0)
    accumulate(n_chunks - 1, 1)


def _flash_scratch(dv, nq, tk):
    wide = nq if (nq // LANES) % 2 else nq + LANES
    return [pltpu.VMEM((1, nq), F32), pltpu.VMEM((dv + ONES_ROWS, nq), F32),
            pltpu.VMEM((2, tk, wide), F32), pltpu.VMEM((2, tk, wide), BF16),
            pltpu.VMEM((2, 1, nq), F32), pltpu.VMEM((2, 1, nq), F32)]


def _key_chunk(s_real, target):
    best = None
    for t in range(LANES, min(s_real // 2, target) + 1, LANES):
        if s_real % (2 * t) == 0:
            best = t
    assert best is not None
    return best


def _mla_attn_kernel(qT_ref, kn_ref, kr_ref, vT_ref, knm_ref, krm_ref, vTm_ref, o_ref, *scratch, tk):
    rows = lambda c: pl.ds(pl.multiple_of(c * tk, tk), tk)
    _flash_t(qT_ref[...],
             lambda c: jnp.concatenate([kn_ref[rows(c), :], kr_ref[rows(c), :]], axis=1),
             lambda c: vT_ref[:, rows(c)],
             lambda: jnp.concatenate([knm_ref[...], krm_ref[...]], axis=1),
             lambda: vTm_ref[...],
             *scratch, n_chunks=kn_ref.shape[0] // tk)
    acc_sc = scratch[1]
    o_ref[...] = (acc_sc[:A_V, :] / acc_sc[A_V:A_V + 1, :]).T.astype(o_ref.dtype)


def _mla_attention(qaT, kn, krot, vaT, s_real, tq, tk):
    lp = qaT.shape[1]
    mb = s_real // LANES
    return pl.pallas_call(
        functools.partial(_mla_attn_kernel, tk=tk),
        out_shape=jax.ShapeDtypeStruct((lp, A_HEADS * A_V), BF16),
        grid=(A_HEADS, lp // tq),
        in_specs=[pl.BlockSpec((A_QK, tq), lambda h, i: (h, i)),
                  pl.BlockSpec((s_real, A_NOPE), lambda h, i: (0, h), pipeline_mode=pl.Buffered(1)),
                  pl.BlockSpec((s_real, LANES), lambda h, i: (0, 0), pipeline_mode=pl.Buffered(1)),
                  pl.BlockSpec((A_V, s_real), lambda h, i: (h, 0), pipeline_mode=pl.Buffered(1)),
                  pl.BlockSpec((LANES, A_NOPE), lambda h, i: (mb, h)),
                  pl.BlockSpec((LANES, LANES), lambda h, i: (mb, 0)),
                  pl.BlockSpec((A_V, LANES), lambda h, i: (h, mb))],
        out_specs=pl.BlockSpec((tq, A_V), lambda h, i: (i, h)),
        scratch_shapes=_flash_scratch(A_V, tq, tk),
        compiler_params=_cparams(("parallel", "parallel")),
        name="mla_attention",
    )(qaT, kn, krot, vaT, kn, krot, vaT)


def _gqa_attn_kernel(qT_ref, k_ref, vT_ref, km_ref, vTm_ref, o_ref, *scratch, tk, n_stack):
    dh = k_ref.shape[1]
    tq = qT_ref.shape[1]
    rows = lambda c: pl.ds(pl.multiple_of(c * tk, tk), tk)
    qT = jnp.concatenate([qT_ref[j * dh:(j + 1) * dh, :] for j in range(n_stack)], axis=1)
    _flash_t(qT, lambda c: k_ref[rows(c), :], lambda c: vT_ref[:, rows(c)], lambda: km_ref[...], lambda: vTm_ref[...],
             *scratch, n_chunks=k_ref.shape[0] // tk)
    acc_sc = scratch[1]
    for j in range(n_stack):
        sl = slice(j * tq, (j + 1) * tq)
        o_ref[:, j * dh:(j + 1) * dh] = (acc_sc[:dh, sl] / acc_sc[dh:dh + 1, sl]).T.astype(o_ref.dtype)


def _gqa_attention(qbT, kb, vbT, s_real, tq, tk):
    lp = qbT.shape[1]
    grp = B_HEADS // B_KV_HEADS
    dh = B_HEAD_DIM
    mb = s_real // LANES
    return pl.pallas_call(
        functools.partial(_gqa_attn_kernel, tk=tk, n_stack=grp),
        out_shape=jax.ShapeDtypeStruct((lp, B_HEADS * dh), BF16),
        grid=(B_KV_HEADS, lp // tq),
        in_specs=[pl.BlockSpec((grp * dh, tq), lambda g, i: (g, i)),
                  pl.BlockSpec((s_real, dh), lambda g, i: (0, g)),
                  pl.BlockSpec((dh, s_real), lambda g, i: (g, 0)),
                  pl.BlockSpec((LANES, dh), lambda g, i: (mb, g)),
                  pl.BlockSpec((dh, LANES), lambda g, i: (g, mb))],
        out_specs=pl.BlockSpec((tq, grp * dh), lambda g, i: (i, g)),
        scratch_shapes=_flash_scratch(dh, grp * tq, tk),
        compiler_params=_cparams(("parallel", "parallel")),
        name="gqa_attention",
    )(qbT, kb, vbT, kb, vbT)


def _out_proj0_kernel(oa_ref, ob_ref, wa_ref, wb_ref, h_ref, gpost_ref, gnext_ref, hn_ref, xn_ref):
    y = _dot(oa_ref[...], wa_ref[...]) + _dot(ob_ref[...], wb_ref[...])
    h = h_ref[...] + _rms(y, gpost_ref[...])
    hn_ref[...] = h
    xn_ref[...] = _rms(h, gnext_ref[...]).astype(xn_ref.dtype)


def _out_proj0(oa, ob, wa, wb, h, gpost, gnext, tm):
    lp, d = h.shape
    row = lambda n: pl.BlockSpec((tm, n), lambda i: (i, 0))
    full = lambda a: pl.BlockSpec(a.shape, lambda i: (0, 0))
    return pl.pallas_call(
        _out_proj0_kernel,
        out_shape=(jax.ShapeDtypeStruct((lp, d), F32), jax.ShapeDtypeStruct((lp, d), BF16)),
        grid=(lp // tm,),
        in_specs=[row(oa.shape[1]), row(ob.shape[1]), full(wa), full(wb), row(d), full(gpost), full(gnext)],
        out_specs=(row(d), row(d)),
        compiler_params=_cparams(("parallel",)),
        name="out_proj0",
    )(oa, ob, wa, wb, h, gpost, gnext)


def _swiglu_accumulate(y_ref, x_ref, wg_ref, wu_ref, wd_ref):
    x = x_ref[...]
    g = _dot(x, wg_ref[...].astype(BF16))
    u = _dot(x, wu_ref[...].astype(BF16))
    a = (g * jax.nn.sigmoid(g) * u).astype(BF16)
    y_ref[...] += _dot(a, wd_ref[...].astype(BF16))


def _ffn0_kernel(x_ref, wg_ref, wu_ref, wd_ref, h_ref, gpost_ref, gnext_ref, hn_ref, xn_ref):
    j = pl.program_id(1)

    @pl.when(j == 0)
    def _():
        hn_ref[...] = jnp.zeros_like(hn_ref)

    _swiglu_accumulate(hn_ref, x_ref, wg_ref, wu_ref, wd_ref)

    @pl.when(j == pl.num_programs(1) - 1)
    def _():
        h = h_ref[...] + _rms(hn_ref[...], gpost_ref[...])
        hn_ref[...] = h
        xn_ref[...] = _rms(h, gnext_ref[...]).astype(xn_ref.dtype)


def _ffn0(xn, wg, wu, wd, h, gpost, gnext, tm, tf):
    lp, d = h.shape
    f = wg.shape[1]
    row = pl.BlockSpec((tm, d), lambda i, j: (i, 0))
    vec = pl.BlockSpec((1, d), lambda i, j: (0, 0))
    return pl.pallas_call(
        _ffn0_kernel,
        out_shape=(jax.ShapeDtypeStruct((lp, d), F32), jax.ShapeDtypeStruct((lp, d), BF16)),
        grid=(lp // tm, f // tf),
        in_specs=[row, pl.BlockSpec((d, tf), lambda i, j: (0, j)), pl.BlockSpec((d, tf), lambda i, j: (0, j)),
                  pl.BlockSpec((tf, d), lambda i, j: (j, 0)), row, vec, vec],
        out_specs=(row, row),
        compiler_params=_cparams(("parallel", "arbitrary")),
        name="ffn0",
    )(xn, wg, wu, wd, h, gpost, gnext)


def _odd_proj_kernel(xn_ref, w_ref, b_ref, cos_ref, sin_ref, q_ref, k_ref, v_ref):
    z = _dot(xn_ref[...], w_ref[...]) + b_ref[...]
    cos, sin = cos_ref[...], sin_ref[...]
    sc = LOG2E / math.sqrt(C_HEAD_DIM)
    nq = q_ref.shape[1]
    nk = k_ref.shape[1]

    def rope(x):
        return x * cos + _half_swap(x, C_ROT // 2) * sin

    for c in range(0, nq, LANES):
        q_ref[:, c:c + LANES] = (rope(z[:, c:c + LANES]) * sc).astype(BF16)
    for c in range(0, nk, LANES):
        k_ref[:, c:c + LANES] = rope(z[:, nq + c:nq + c + LANES]).astype(BF16)
    v_ref[...] = z[:, nq + nk:].astype(BF16)


def _odd_proj(xn, w_p, b_p, cos, sin, tm):
    lp, d = xn.shape
    nq = C_HEADS * C_HEAD_DIM
    nk = C_KV_HEADS * LANES
    row = lambda n: pl.BlockSpec((tm, n), lambda i: (i, 0))
    full = lambda a: pl.BlockSpec(a.shape, lambda i: (0, 0))
    return pl.pallas_call(
        _odd_proj_kernel,
        out_shape=(jax.ShapeDtypeStruct((lp, nq), BF16), jax.ShapeDtypeStruct((lp, nk), BF16),
                   jax.ShapeDtypeStruct((lp, nk), BF16)),
        grid=(lp // tm,),
        in_specs=[row(d), full(w_p), full(b_p), row(LANES), row(LANES)],
        out_specs=(row(nq), row(nk), row(nk)),
        compiler_params=_cparams(("parallel",)),
        name="odd_proj",
    )(xn, w_p, b_p, cos, sin)


def _win_attn_kernel(sink_ref, q_ref, k0, k1, k2, k3, km, v0, v1, v2, v3, vm, o_ref, *, s_real, tq):
    g = pl.program_id(0)
    i = pl.program_id(1)
    is_meta = i == pl.num_programs(1) - 1
    nband = 4 * LANES
    kcat = jnp.concatenate([k0[...], k1[...], k2[...], k3[...], km[...]], axis=0)
    vcat = jnp.concatenate([v0[...], v1[...], v2[...], v3[...], vm[...]], axis=0)
    ncol = kcat.shape[0]
    a = lax.broadcasted_iota(jnp.int32, (tq, ncol), 0)
    c = lax.broadcasted_iota(jnp.int32, (tq, ncol), 1)
    kidx = jnp.where(is_meta, 0, i * tq - LANES) + c
    qidx = i * tq + a
    ninf = jnp.float32(-jnp.inf)
    b_real = jnp.where(jnp.abs(qidx - kidx) <= C_WINDOW, 0.0, ninf)
    b_real = jnp.where(kidx >= 0, b_real, ninf)
    b_real = jnp.where(kidx < s_real, b_real, ninf)
    b_meta_q = jnp.where(kidx <= (C_WINDOW - N_META) + a, 0.0, ninf)
    b_meta_k = jnp.where(c - nband < N_META, 0.0, ninf)
    bias = jnp.where(c < nband, jnp.where(is_meta, b_meta_q, b_real), b_meta_k)
    lane = lax.broadcasted_iota(jnp.int32, (tq, LANES), 1)
    heads_per_group = C_HEADS // C_KV_HEADS
    for pr in range(heads_per_group // 2):
        qp = q_ref[:, pr * LANES:(pr + 1) * LANES]
        halves = []
        for half in range(2):
            sink = sink_ref[g * heads_per_group + 2 * pr + half] * LOG2E
            qm = jnp.where((lane >= C_HEAD_DIM) == (half == 1), qp, jnp.zeros_like(qp))
            s = _dot_nt(qm, kcat) + bias
            m = jnp.maximum(jnp.max(s, axis=1, keepdims=True), sink)
            e = jnp.exp2(s - m)
            denom = jnp.sum(e, axis=1, keepdims=True) + jnp.exp2(sink - m)
            halves.append(_dot(e.astype(BF16), vcat) / denom)
        o_ref[:, pr * LANES:(pr + 1) * LANES] = jnp.where(lane < C_HEAD_DIM, halves[0], halves[1]).astype(o_ref.dtype)


def _win_attention(qc, kc, vc, sink, s_real, tq):
    lp = qc.shape[0]
    nq = lp // tq
    nkb = s_real // LANES
    per = tq // LANES
    gw = (C_HEADS // C_KV_HEADS) * C_HEAD_DIM

    def band(k):
        def index(g, i, sink_ref):
            b = jnp.where(i == nq - 1, k, jnp.clip(i * per - 1 + k, 0, nkb - 1))
            return (b, g)
        return pl.BlockSpec((LANES, LANES), index)

    meta = pl.BlockSpec((LANES, LANES), lambda g, i, sink_ref: (nkb, g))
    grid_spec = pltpu.PrefetchScalarGridSpec(
        num_scalar_prefetch=1,
        grid=(C_KV_HEADS, nq),
        in_specs=[pl.BlockSpec((tq, gw), lambda g, i, sink_ref: (i, g))]
                 + [band(k) for k in range(4)] + [meta] + [band(k) for k in range(4)] + [meta],
        out_specs=pl.BlockSpec((tq, gw), lambda g, i, sink_ref: (i, g)),
    )
    return pl.pallas_call(
        functools.partial(_win_attn_kernel, s_real=s_real, tq=tq),
        out_shape=jax.ShapeDtypeStruct((lp, C_HEADS * C_HEAD_DIM), BF16),
        grid_spec=grid_spec,
        compiler_params=_cparams(("parallel", "parallel")),
        name="win_attention",
    )(sink, qc, kc, kc, kc, kc, kc, vc, vc, vc, vc, vc)


def _out_proj1_kernel(o_ref, w_ref, b_ref, h_ref, gpost_ref, gnext_ref, router_ref, hn_ref, xn_ref, idx_ref, gate_ref):
    y = _dot(o_ref[...], w_ref[...]) + b_ref[...]
    h = h_ref[...] + _rms(y, gpost_ref[...])
    hn_ref[...] = h
    xn = _rms(h, gnext_ref[...])
    xn_ref[...] = xn
    logit = [jnp.sum(xn * router_ref[e:e + 1, :], axis=1, keepdims=True) for e in range(N_EXPERTS)]

    def top(cols):
        v = functools.reduce(jnp.maximum, cols)
        i = jnp.full_like(v, N_EXPERTS, dtype=jnp.int32)
        for e in reversed(range(N_EXPERTS)):
            i = jnp.where(cols[e] == v, e, i)
        return v, i

    v1, i1 = top(logit)
    v2, i2 = top([jnp.where(i1 == e, -jnp.inf, logit[e]) for e in range(N_EXPERTS)])
    e2 = jnp.exp(v2 - v1)
    g1 = 1.0 / (1.0 + e2)
    g2 = e2 / (1.0 + e2)
    lane = lax.broadcasted_iota(jnp.int32, idx_ref.shape, 1)
    idx_ref[...] = jnp.where(lane == 0, i1, jnp.where(lane == 1, i2, 0))
    gate_ref[...] = jnp.where(lane == 0, g1, jnp.where(lane == 1, g2, 0.0))


def _out_proj1(o, w, b, h, gpost, gnext, router_p, tm):
    lp, d = h.shape
    row = lambda n: pl.BlockSpec((tm, n), lambda i: (i, 0))
    full = lambda a: pl.BlockSpec(a.shape, lambda i: (0, 0))
    return pl.pallas_call(
        _out_proj1_kernel,
        out_shape=(jax.ShapeDtypeStruct((lp, d), F32), jax.ShapeDtypeStruct((lp, d), F32),
                   jax.ShapeDtypeStruct((lp, LANES), jnp.int32), jax.ShapeDtypeStruct((lp, LANES), F32)),
        grid=(lp // tm,),
        in_specs=[row(o.shape[1]), full(w), full(b), row(d), full(gpost), full(gnext), full(router_p)],
        out_specs=(row(d), row(d), row(LANES), row(LANES)),
        compiler_params=_cparams(("parallel",)),
        name="out_proj1",
    )(o, w, b, h, gpost, gnext, router_p)


def _gather_kernel(tok_ref, x_hbm, o_ref, buf, sem, *, tg):
    i = pl.program_id(0)
    n = pl.num_programs(0)

    def row_copy(step, slot, r):
        return pltpu.make_async_copy(x_hbm.at[pl.ds(tok_ref[step * tg + r], 1)], buf.at[slot, pl.ds(r, 1)],
                                     sem.at[slot])

    def start_all(step, slot):
        def body(r2, c):
            row_copy(step, slot, 2 * r2).start(priority=0)
            row_copy(step, slot, 2 * r2 + 1).start(priority=1)
            return c
        lax.fori_loop(0, tg // 2, body, 0, unroll=4)

    def wait_all(step, slot):
        def body(r, c):
            row_copy(step, slot, r).wait()
            return c
        lax.fori_loop(0, tg, body, 0, unroll=8)

    @pl.when(i == 0)
    def _():
        start_all(0, 0)

    for slot in range(2):
        @pl.when((i % 2 == slot) & (i + 1 < n))
        def _():
            start_all(i + 1, 1 - slot)

        @pl.when(i % 2 == slot)
        def _():
            wait_all(i, slot)
            o_ref[...] = buf[slot].astype(o_ref.dtype)


def _gather_rows(row_tok, x, tg):
    n_rows = row_tok.shape[0]
    d = x.shape[1]
    grid_spec = pltpu.PrefetchScalarGridSpec(
        num_scalar_prefetch=1,
        grid=(n_rows // tg,),
        in_specs=[pl.BlockSpec(memory_space=pl.ANY)],
        out_specs=pl.BlockSpec((tg, d), lambda i, tok: (i, 0)),
        scratch_shapes=[pltpu.VMEM((2, tg, d), x.dtype), pltpu.SemaphoreType.DMA((2,))],
    )
    return pl.pallas_call(
        functools.partial(_gather_kernel, tg=tg),
        out_shape=jax.ShapeDtypeStruct((n_rows, d), BF16),
        grid_spec=grid_spec,
        compiler_params=_cparams(("arbitrary",)),
        name="moe_gather",
    )(row_tok, x)


def _moe_ffn_kernel(te_ref, nu_ref, tok_ref, x_hbm, wg_ref, wu_ref, wd_ref, y_ref, stage, xb, sem):
    i = pl.program_id(0)
    j = pl.program_id(1)
    tm = y_ref.shape[0]
    n_used = nu_ref[0]

    def row_copy(t, r):
        return pltpu.make_async_copy(x_hbm.at[pl.ds(tok_ref[t * tm + r], 1)], stage.at[pl.ds(r, 1)], sem)

    def start_all(t):
        def body(r2, c):
            row_copy(t, 2 * r2).start(priority=0)
            row_copy(t, 2 * r2 + 1).start(priority=1)
            return c
        lax.fori_loop(0, tm // 2, body, 0, unroll=4)

    def wait_all(t):
        def body(r, c):
            row_copy(t, r).wait()
            return c
        lax.fori_loop(0, tm, body, 0, unroll=8)

    @pl.when(j == 0)
    def _():
        y_ref[...] = jnp.zeros_like(y_ref)

    @pl.when((j == 0) & (i == 0) & (n_used > 0))
    def _():
        start_all(0)

    @pl.when((j == 0) & (i < n_used))
    def _():
        wait_all(i)
        xb[...] = stage[...].astype(xb.dtype)

    @pl.when((j == 0) & (i + 1 < n_used))
    def _():
        start_all(i + 1)

    @pl.when(i < n_used)
    def _():
        _swiglu_accumulate(y_ref, xb, wg_ref.at[0], wu_ref.at[0], wd_ref.at[0])


def _moe_ffn(tile_expert, n_used, row_tok, x, wg, wu, wd, tm, tf):
    n_rows = row_tok.shape[0]
    d = x.shape[1]
    f = wg.shape[2]
    nf = f // tf

    def jj(i, j, nu):
        return jnp.where(i < nu[0], j, nf - 1)

    grid_spec = pltpu.PrefetchScalarGridSpec(
        num_scalar_prefetch=3,
        grid=(n_rows // tm, nf),
        in_specs=[pl.BlockSpec(memory_space=pl.ANY),
                  pl.BlockSpec((1, d, tf), lambda i, j, te, nu, tok: (te[i], 0, jj(i, j, nu))),
                  pl.BlockSpec((1, d, tf), lambda i, j, te, nu, tok: (te[i], 0, jj(i, j, nu))),
                  pl.BlockSpec((1, tf, d), lambda i, j, te, nu, tok: (te[i], jj(i, j, nu), 0))],
        out_specs=pl.BlockSpec((tm, d), lambda i, j, te, nu, tok: (i, 0)),
        scratch_shapes=[pltpu.VMEM((tm, d), x.dtype), pltpu.VMEM((tm, d), BF16), pltpu.SemaphoreType.DMA(())],
    )
    return pl.pallas_call(
        _moe_ffn_kernel,
        out_shape=jax.ShapeDtypeStruct((n_rows, d), F32),
        grid_spec=grid_spec,
        compiler_params=_cparams(("arbitrary", "arbitrary")),
        name="moe_ffn",
    )(tile_expert, n_used, row_tok, x, wg, wu, wd)


def _combine_kernel(dest_ref, y_hbm, h_ref, gate_ref, gpost_ref, o_ref, buf, sem, *, tc):
    i = pl.program_id(0)
    n = pl.num_programs(0)

    def row_copy(step, slot, r, k):
        return pltpu.make_async_copy(y_hbm.at[pl.ds(dest_ref[(step * tc + r) * TOP_K + k], 1)],
                                     buf.at[slot, k, pl.ds(r, 1)], sem.at[slot])

    def start_all(step, slot):
        def body(r, c):
            row_copy(step, slot, r, 0).start(priority=0)
            row_copy(step, slot, r, 1).start(priority=1)
            return c
        lax.fori_loop(0, tc, body, 0, unroll=4)

    def wait_all(step, slot):
        def body(r, c):
            row_copy(step, slot, r, 0).wait()
            row_copy(step, slot, r, 1).wait()
            return c
        lax.fori_loop(0, tc, body, 0, unroll=4)

    @pl.when(i == 0)
    def _():
        start_all(0, 0)

    for slot in range(2):
        @pl.when((i % 2 == slot) & (i + 1 < n))
        def _():
            start_all(i + 1, 1 - slot)

        @pl.when(i % 2 == slot)
        def _():
            wait_all(i, slot)
            gate = gate_ref[...]
            y = buf[slot, 0] * gate[:, 0:1] + buf[slot, 1] * gate[:, 1:2]
            o_ref[...] = h_ref[...] + _rms(y, gpost_ref[...])


def _combine(dest_flat, ys, h, gates, gpost, s_real, tc):
    d = h.shape[1]
    grid_spec = pltpu.PrefetchScalarGridSpec(
        num_scalar_prefetch=1,
        grid=(s_real // tc,),
        in_specs=[pl.BlockSpec(memory_space=pl.ANY),
                  pl.BlockSpec((tc, d), lambda i, dest: (i, 0)),
                  pl.BlockSpec((tc, LANES), lambda i, dest: (i, 0)),
                  pl.BlockSpec((1, d), lambda i, dest: (0, 0))],
        out_specs=pl.BlockSpec((tc, d), lambda i, dest: (i, 0)),
        scratch_shapes=[pltpu.VMEM((2, TOP_K, tc, d), ys.dtype), pltpu.SemaphoreType.DMA((2,))],
    )
    return pl.pallas_call(
        functools.partial(_combine_kernel, tc=tc),
        out_shape=jax.ShapeDtypeStruct((s_real, d), F32),
        grid_spec=grid_spec,
        compiler_params=_cparams(("arbitrary",)),
        name="moe_combine",
    )(dest_flat, ys, h, gates, gpost)


def _rope_angles(pos, dim, theta):
    inv_freq = theta ** (-jnp.arange(0, dim, 2, dtype=F32) / dim)
    return pos.astype(F32)[:, None] * inv_freq[None, :]


def _tables(s_real, lp):
    r = jnp.arange(lp, dtype=jnp.int32)
    is_real = r < s_real
    p = r - s_real
    pos = jnp.where(is_real, r + N_META, jnp.clip(p, 0, N_META - 1))
    grid_row = jnp.where(is_real, r // GRID_W, -1)
    grid_col = jnp.where(is_real, r % GRID_W, jnp.clip(p, 0, N_META - 1))

    def cs(ang):
        c, s = jnp.cos(ang), jnp.sin(ang)
        return jnp.concatenate([c, c], axis=1), jnp.concatenate([-s, s], axis=1)

    a_c, a_s = cs(_rope_angles(pos, A_ROPE, ROPE_THETA))
    pad_c = jnp.ones((lp, LANES - A_ROPE), F32)
    acos = jnp.concatenate([a_c, pad_c], axis=1)
    asin = jnp.concatenate([a_s, 0.0 * pad_c], axis=1)
    r_c, r_s = cs(_rope_angles(grid_row, B_HEAD_DIM // 2, AXIAL_THETA))
    c_c, c_s = cs(_rope_angles(grid_col, B_HEAD_DIM // 2, AXIAL_THETA))
    bcos = jnp.concatenate([r_c, c_c], axis=1)
    bsin = jnp.concatenate([r_s, c_s], axis=1)
    w_c, w_s = cs(_rope_angles(pos, C_ROT, ROPE_THETA))
    one = jnp.ones((lp, C_HEAD_DIM - C_ROT), F32)
    ccos = jnp.concatenate([w_c, one, w_c, one], axis=1)
    csin = jnp.concatenate([w_s, 0.0 * one, w_s, 0.0 * one], axis=1)
    return acos, asin, bcos, bsin, ccos, csin


def _route(idx, tm, n_tiles):
    e_flat = idx.reshape(-1)
    n_assign = e_flat.shape[0]
    experts = jnp.arange(N_EXPERTS, dtype=jnp.int32)
    onehot = (e_flat[:, None] == experts[None, :]).astype(jnp.int32)
    csum = jnp.cumsum(onehot, axis=0)
    counts = csum[-1]
    padded = (counts + tm - 1) // tm * tm
    start = jnp.cumsum(counts) - counts
    pend = jnp.cumsum(padded)
    pstart = pend - padded
    dest_flat = jnp.sum(onehot * (pstart[None, :] + csum - 1), axis=1)
    order = jnp.argsort(e_flat * n_assign + jnp.arange(n_assign, dtype=jnp.int32))
    rows = jnp.arange(n_tiles * tm, dtype=jnp.int32)
    row_expert = jnp.minimum(jnp.searchsorted(pend, rows, side='right'), N_EXPERTS - 1)
    rank = rows - pstart[row_expert]
    assign = order[jnp.clip(start[row_expert] + rank, 0, n_assign - 1)]
    row_tok = jnp.where(rank < counts[row_expert], assign // TOP_K, rows % (n_assign // TOP_K)).astype(jnp.int32)
    tile_expert = row_expert[::tm].astype(jnp.int32)
    n_used = (pend[-1] // tm).astype(jnp.int32).reshape(1)
    return row_tok, dest_flat, tile_expert, n_used


def kernel(x, meta_tokens, even_attn_pre, even_attn_post, even_w_in, a_q_norm, a_wq_b, a_kv_norm, a_wkv_b, b_q_norm, b_k_norm, even_w_out, even_ffn_pre, even_ffn_post, ffn_w_gate, ffn_w_up, ffn_w_down, odd_attn_pre, odd_attn_post, c_w_qkv, c_b_qkv, c_sink, c_w_out, c_b_out, odd_ffn_pre, odd_ffn_post, moe_router, moe_w_gate, moe_w_up, moe_w_down):
    bsz, s_real, d = x.shape
    assert bsz == 1 and s_real % ROW_PAD == 0 and s_real % GRID_W == 0
    lp = s_real + ROW_PAD
    n_tok = s_real + N_META
    vec = lambda g: g.reshape(1, -1).astype(F32)
    acos, asin, bcos, bsin, ccos, csin = _tables(s_real, lp)
    tm_proj = _row_tile(lp, 640)

    h, xn = _prep(x[0], meta_tokens, vec(even_attn_pre[0]), lp)

    w_in = even_w_in[0]
    w_in_p = jnp.concatenate([w_in[:, :A_IN], jnp.zeros((d, LANES - A_ROPE), F32), w_in[:, A_IN:]], axis=1).astype(BF16)
    wq_p = jnp.pad(a_wq_b[0].reshape(A_Q_RANK, A_HEADS, A_NOPE + A_ROPE),
                   ((0, 0), (0, 0), (0, A_QK - A_NOPE - A_ROPE))).reshape(A_Q_RANK, A_HEADS * A_QK).astype(BF16)
    qaT, kn, vaT, krot, qbT, kb, vbT = _even_proj(
        xn, w_in_p, vec(a_q_norm[0]), wq_p, vec(a_kv_norm[0]), a_wkv_b[0].astype(BF16),
        vec(b_q_norm[0]), vec(b_k_norm[0]), acos, asin, bcos, bsin, _row_tile(lp, 256))

    tk = _key_chunk(s_real, 2048)
    oa = _mla_attention(qaT, kn, krot, vaT, s_real, _row_tile(lp, 1280), tk)
    ob = _gqa_attention(qbT, kb, vbT, s_real, _row_tile(lp, 256), tk)

    w_out = even_w_out[0].astype(BF16)
    n_a = A_HEADS * A_V
    h, xn = _out_proj0(oa, ob, w_out[:n_a], w_out[n_a:], h, vec(even_attn_post[0]), vec(even_ffn_pre[0]), tm_proj)

    tf = 512 if ffn_w_gate.shape[2] % 512 == 0 else LANES
    h, xn = _ffn0(xn, ffn_w_gate[0].astype(BF16), ffn_w_up[0].astype(BF16), ffn_w_down[0].astype(BF16),
                  h, vec(even_ffn_post[0]), vec(odd_attn_pre[0]), _row_tile(lp, 640), tf)

    nq = C_HEADS * C_HEAD_DIM
    nkv = C_KV_HEADS * C_HEAD_DIM

    def dup_heads(w):
        w4 = w.reshape(w.shape[:-1] + (C_KV_HEADS, C_HEAD_DIM))
        return jnp.concatenate([w4, w4], axis=-1).reshape(w.shape[:-1] + (C_KV_HEADS * LANES,))

    wc, bc = c_w_qkv[0], c_b_qkv[0]
    wc_p = jnp.concatenate([wc[:, :nq], dup_heads(wc[:, nq:nq + nkv]), dup_heads(wc[:, nq + nkv:])], axis=1).astype(BF16)
    bc_p = jnp.concatenate([bc[:nq], dup_heads(bc[nq:nq + nkv]), dup_heads(bc[nq + nkv:])]).reshape(1, -1)
    qc, kc, vc = _odd_proj(xn, wc_p, bc_p, ccos, csin, tm_proj)
    oc = _win_attention(qc, kc, vc, c_sink[0].astype(F32), s_real, ROW_PAD)

    h, xn_f32, idx_l, gate_l = _out_proj1(oc, c_w_out[0].astype(BF16), vec(c_b_out[0]), h, vec(odd_attn_post[0]),
                                          vec(odd_ffn_pre[0]), moe_router[0].T, tm_proj)

    tm_moe = 1024 if s_real >= 8192 else 256
    n_assign = n_tok * TOP_K
    n_tiles = -(-(n_assign + N_EXPERTS * (tm_moe - 1)) // tm_moe)
    row_tok, dest_flat, tile_expert, n_used = _route(idx_l[:n_tok, :TOP_K], tm_moe, n_tiles)
    tf_moe = 512 if moe_w_gate.shape[3] % 512 == 0 else LANES
    ys = _moe_ffn(tile_expert, n_used, row_tok, xn_f32, moe_w_gate[0], moe_w_up[0], moe_w_down[0], tm_moe, tf_moe)
    out = _combine(dest_flat, ys, h, gate_l, vec(odd_ffn_post[0]), s_real, 256)
    return out[None]
```
